```python
import math
import jax, jax.numpy as jnp
from jax import lax
import numpy as np

D_MODEL = 4096
BATCH = 4
SEQ = 2048
DEPTH = 2
DEC_BATCH = 8
DEC_SEQ = 8
PAST_LEN = 16384
PAGE_SIZE = 128

N_MIXERS = 2
N_ATTN_LAYERS = (DEPTH + 1) // 2
N_SSM_LAYERS = DEPTH // 2
ATTN_HEAD_DIM = 128
ATTN_HEADS = D_MODEL // (2 * ATTN_HEAD_DIM)
ATTN_WIDTH = 2 * ATTN_HEADS * ATTN_HEAD_DIM
Q_BLOCK = 128
SUBLN_EPS = 1e-5
N_BUCKETS = 32
MAX_DISTANCE = 128
SSM_INNER = 2 * D_MODEL
SSM_HEAD_DIM = 64
SSM_HEADS = SSM_INNER // SSM_HEAD_DIM
SSM_GROUPS = 8
SSM_STATE = 128
CONV_WIDTH = 4
CONV_DIM = SSM_INNER + 2 * SSM_GROUPS * SSM_STATE
SSM_IN_WIDTH = 2 * SSM_INNER + 2 * SSM_GROUPS * SSM_STATE + SSM_HEADS
SSD_CHUNK = 128
NORM_EPS = 1e-6

kernel_name = "hybrid_diffattn_ssd_decode_step"


def rms_norm(x, w, eps=NORM_EPS):
    xf = x.astype(jnp.float32)
    y = xf * lax.rsqrt(jnp.mean(xf * xf, axis=-1, keepdims=True) + eps)
    return (y * w.astype(jnp.float32)).astype(x.dtype)


def t5_bucket(rel):
    n = jnp.maximum(-rel, 0)
    max_exact = N_BUCKETS // 2
    large = max_exact + (jnp.log(jnp.maximum(n, max_exact).astype(jnp.float32) / max_exact)
                         / math.log(MAX_DISTANCE / max_exact) * (N_BUCKETS - max_exact)).astype(jnp.int32)
    large = jnp.minimum(large, N_BUCKETS - 1)
    return jnp.where(n < max_exact, n, large)


def diff_attend(q, k, v, q_pos, k_pos, lam, rel_table):
    b, lq = q.shape[:2]
    lk = k.shape[1]
    s = jnp.einsum('bqcd,bkcd->bcqk', q, k).astype(jnp.float32) * (ATTN_HEAD_DIM ** -0.5)
    s = s.reshape(b, ATTN_HEADS, 2, lq, lk)
    rel = k_pos[None, :] - q_pos[:, None]
    bias = jnp.moveaxis(rel_table.astype(jnp.float32)[t5_bucket(rel)], -1, 0)
    s = s + bias[None, :, None]
    s = jnp.where((rel <= 0)[None, None, None], s, -jnp.inf)
    p = jax.nn.softmax(s, axis=-1)
    a = p[:, :, 0] - lam * p[:, :, 1]
    return jnp.einsum('bhqk,bkhe->bqhe', a.astype(v.dtype), v)


def diff_attn_project(h, w_in):
    b, l, _ = h.shape
    q, k, v, g = jnp.split(h @ w_in, 4, axis=-1)
    return (q.reshape(b, l, 2 * ATTN_HEADS, ATTN_HEAD_DIM),
            k.reshape(b, l, 2 * ATTN_HEADS, ATTN_HEAD_DIM),
            v.reshape(b, l, ATTN_HEADS, 2 * ATTN_HEAD_DIM), g)


def diff_attn_output(o, g, subln_w, lam_init, w_out):
    b, l = o.shape[:2]
    o = rms_norm(o, subln_w, SUBLN_EPS) * (1.0 - lam_init)
    return (o.reshape(b, l, ATTN_WIDTH) * jax.nn.silu(g)) @ w_out


def ssd_scan(x, dt, A, B, C, state0):
    b, l, nh, p = x.shape
    g, n = B.shape[2:]
    r = nh // g
    q = SSD_CHUNK if l % SSD_CHUNK == 0 else l
    nc = l // q

    def to_chunks(t):
        return jnp.moveaxis(t.reshape((b, nc, q) + t.shape[2:]), 1, 0)

    xc = to_chunks(x.astype(jnp.float32).reshape(b, l, g, r, p))
    dtc = to_chunks(dt.reshape(b, l, g, r))
    Bc = to_chunks(B.astype(jnp.float32))
    Cc = to_chunks(C.astype(jnp.float32))
    Ag = A.reshape(g, r)
    causal = jnp.tril(jnp.ones((q, q), dtype=bool))

    def step(state, inp):
        xk, dtk, Bk, Ck = inp
        cum = jnp.cumsum(dtk * Ag, axis=1)
        seg = cum[:, :, None] - cum[:, None, :]
        decay = jnp.exp(jnp.where(causal[None, :, :, None, None], seg, -jnp.inf))
        cb = jnp.einsum('blgn,bsgn->blsg', Ck, Bk)
        y = jnp.einsum('blsgr,bsgrp->blgrp', cb[..., None] * decay * dtk[:, None], xk)
        y = y + jnp.einsum('blgn,bgrpn->blgrp', Ck, state) * jnp.exp(cum)[..., None]
        w_end = jnp.exp(cum[:, -1:] - cum) * dtk
        state = (state * jnp.exp(cum[:, -1])[..., None, None]
                 + jnp.einsum('bsgr,bsgrp,bsgn->bgrpn', w_end, xk, Bk))
        return state, y

    state, yc = lax.scan(step, state0.astype(jnp.float32).reshape(b, g, r, p, n), (xc, dtc, Bc, Cc))
    y = jnp.moveaxis(yc, 0, 1).reshape(b, l, nh, p)
    return y.astype(x.dtype), state.reshape(b, nh, p, n).astype(state0.dtype)


def ssd_mixer(h, conv_state, ssm_state, w_in, conv_w, conv_b, dt_bias, a_log, d_skip, gnorm_w, w_out):
    b, l, _ = h.shape
    proj = h @ w_in
    z = proj[..., :SSM_INNER]
    xbc = proj[..., SSM_INNER:SSM_INNER + CONV_DIM]
    dt = proj[..., SSM_INNER + CONV_DIM:]
    xpad = jnp.concatenate([conv_state.astype(xbc.dtype), xbc], axis=1)
    conv = conv_b
    for j in range(CONV_WIDTH):
        conv = conv + xpad[:, j:j + l] * conv_w[j]
    xbc = jax.nn.silu(conv)
    xs = xbc[..., :SSM_INNER].reshape(b, l, SSM_HEADS, SSM_HEAD_DIM)
    Bm = xbc[..., SSM_INNER:SSM_INNER + SSM_GROUPS * SSM_STATE].reshape(b, l, SSM_GROUPS, SSM_STATE)
    Cm = xbc[..., SSM_INNER + SSM_GROUPS * SSM_STATE:].reshape(b, l, SSM_GROUPS, SSM_STATE)
    dt = jax.nn.softplus(dt.astype(jnp.float32) + dt_bias.astype(jnp.float32))
    A = -jnp.exp(a_log.astype(jnp.float32))
    y, new_state = ssd_scan(xs, dt, A, Bm, Cm, ssm_state)
    y = y + xs * d_skip[:, None].astype(xs.dtype)
    y = y.reshape(b, l, SSM_INNER) * jax.nn.silu(z)
    y = rms_norm(y.reshape(b, l, SSM_GROUPS, -1), gnorm_w.reshape(SSM_GROUPS, -1)).reshape(b, l, SSM_INNER)
    return y @ w_out, xpad[:, -(CONV_WIDTH - 1):], new_state


def setup_inputs(seed: int = 0) -> dict:
    key = jax.random.key(seed)
    ks = jax.random.split(key, 24)
    f32 = jnp.float32

    def nrm(k, shape, scale):
        return scale * jax.random.normal(k, shape, f32)

    n_pages = PAST_LEN // PAGE_SIZE
    n_used = DEC_BATCH * n_pages
    n_phys = n_used + max(1, n_used // 4)
    page_table = jax.random.permutation(ks[4], n_phys)[:n_used].reshape(DEC_BATCH, n_pages).astype(jnp.int32)
    dt0 = jnp.exp(jax.random.uniform(ks[15], (N_SSM_LAYERS, SSM_HEADS), f32, math.log(1e-3), math.log(1e-1)))
    return {
        'x_prompt': nrm(ks[0], (BATCH, SEQ, D_MODEL), 1.0),
        'x_sample': nrm(ks[1], (DEC_BATCH, DEC_SEQ, D_MODEL), 1.0),
        'cache_k': nrm(ks[2], (N_ATTN_LAYERS, n_phys, PAGE_SIZE, 2 * ATTN_HEADS, ATTN_HEAD_DIM), 1.0),
        'cache_v': nrm(ks[3], (N_ATTN_LAYERS, n_phys, PAGE_SIZE, ATTN_HEADS, 2 * ATTN_HEAD_DIM), 1.0),
        'page_table': page_table,
        'state_conv': nrm(ks[5], (N_SSM_LAYERS, DEC_BATCH, CONV_WIDTH - 1, CONV_DIM), 1.0),
        'state_ssm': nrm(ks[6], (N_SSM_LAYERS, DEC_BATCH, SSM_HEADS, SSM_HEAD_DIM, SSM_STATE), 0.3),
        'norm_pre': 1.0 + nrm(ks[7], (DEPTH, D_MODEL), 0.05),
        'norm_post': 1.0 + nrm(ks[8], (DEPTH, D_MODEL), 0.05),
        'rel_bias': nrm(ks[9], (N_BUCKETS, ATTN_HEADS), 0.5),
        'w_attn_in': nrm(ks[10], (N_ATTN_LAYERS, D_MODEL, 4 * ATTN_WIDTH), D_MODEL ** -0.5),
        'lambda_qk': nrm(ks[11], (N_ATTN_LAYERS, 4, ATTN_HEAD_DIM), 0.1),
        'subln_w': 1.0 + nrm(ks[12], (N_ATTN_LAYERS, 2 * ATTN_HEAD_DIM), 0.05),
        'w_attn_out': nrm(ks[13], (N_ATTN_LAYERS, ATTN_WIDTH, D_MODEL), ATTN_WIDTH ** -0.5),
        'w_ssm_in': nrm(ks[14], (N_SSM_LAYERS, D_MODEL, SSM_IN_WIDTH), D_MODEL ** -0.5),
        'conv_w': nrm(ks[16], (N_SSM_LAYERS, CONV_WIDTH, CONV_DIM), CONV_WIDTH ** -0.5),
        'conv_b': nrm(ks[17], (N_SSM_LAYERS, CONV_DIM), 0.02),
        'dt_bias': dt0 + jnp.log(-jnp.expm1(-dt0)),
        'a_log': jnp.log(jax.random.uniform(ks[18], (N_SSM_LAYERS, SSM_HEADS), f32, 1.0, 16.0)),
        'd_skip': 1.0 + nrm(ks[19], (N_SSM_LAYERS, SSM_HEADS), 0.1),
        'gnorm_w': 1.0 + nrm(ks[20], (N_SSM_LAYERS, SSM_INNER), 0.05),
        'w_ssm_out': nrm(ks[21], (N_SSM_LAYERS, SSM_INNER, D_MODEL), SSM_INNER ** -0.5),
    }


def reference(x_prompt, x_sample, cache_k, cache_v, page_table, state_conv, state_ssm,
              norm_pre, norm_post, rel_bias, w_attn_in, lambda_qk, subln_w, w_attn_out,
              w_ssm_in, conv_w, conv_b, dt_bias, a_log, d_skip, gnorm_w, w_ssm_out):
    bp, sp = x_prompt.shape[:2]
    bs, ss = x_sample.shape[:2]
    past = page_table.shape[1] * cache_k.shape[2]
    xp, xs = x_prompt, x_sample
    kp_l, vp_l, ks_l, vs_l = [], [], [], []
    cp_l, sp_l, cs_l, ss_l = [], [], [], []
    for i in range(DEPTH):
        hp = rms_norm(xp, norm_pre[i])
        hs = rms_norm(xs, norm_pre[i])
        if i % N_MIXERS == 0:
            la = i // N_MIXERS
            lam_init = 0.8 - 0.6 * math.exp(-0.3 * i)
            lq = lambda_qk[la].astype(jnp.float32)
            lam = jnp.exp(jnp.sum(lq[0] * lq[1])) - jnp.exp(jnp.sum(lq[2] * lq[3])) + lam_init
            q, k, v, g = diff_attn_project(hp, w_attn_in[la])
            nqb = sp // Q_BLOCK
            qpos_blocks = jnp.arange(sp).reshape(nqb, Q_BLOCK)
            q_blocks = jnp.moveaxis(q.reshape(bp, nqb, Q_BLOCK, 2 * ATTN_HEADS, ATTN_HEAD_DIM), 1, 0)
            kpos = jnp.arange(sp)
            o = lax.map(lambda a: diff_attend(a[0], k, v, a[1], kpos, lam, rel_bias), (q_blocks, qpos_blocks))
            o = jnp.moveaxis(o, 0, 1).reshape(bp, sp, ATTN_HEADS, 2 * ATTN_HEAD_DIM)
            op = diff_attn_output(o, g, subln_w[la], lam_init, w_attn_out[la])
            kp_l.append(k)
            vp_l.append(v)
            q2, k2, v2, g2 = diff_attn_project(hs, w_attn_in[la])
            qpos_s = past + jnp.arange(ss)
            kpos_s = jnp.arange(past + ss)

            def one_seq(a, la=la):
                qi, ki, vi, pti = a
                kk = jnp.concatenate([cache_k[la, pti].reshape(past, 2 * ATTN_HEADS, ATTN_HEAD_DIM).astype(ki.dtype), ki], axis=0)
                vv = jnp.concatenate([cache_v[la, pti].reshape(past, ATTN_HEADS, 2 * ATTN_HEAD_DIM).astype(vi.dtype), vi], axis=0)
                return diff_attend(qi[None], kk[None], vv[None], qpos_s, kpos_s, lam, rel_bias)[0]

            o2 = lax.map(one_seq, (q2, k2, v2, page_table))
            os_ = diff_attn_output(o2, g2, subln_w[la], lam_init, w_attn_out[la])
            ks_l.append(k2)
            vs_l.append(v2)
        else:
            ls = i // N_MIXERS
            wts = (w_ssm_in[ls], conv_w[ls], conv_b[ls], dt_bias[ls], a_log[ls], d_skip[ls], gnorm_w[ls], w_ssm_out[ls])
            conv0 = jnp.zeros((bp, CONV_WIDTH - 1, CONV_DIM), x_prompt.dtype)
            ssm0 = jnp.zeros((bp, SSM_HEADS, SSM_HEAD_DIM, SSM_STATE), x_prompt.dtype)
            op, cpn, spn = ssd_mixer(hp, conv0, ssm0, *wts)
            os_, csn, ssn = ssd_mixer(hs, state_conv[ls], state_ssm[ls], *wts)
            cp_l.append(cpn)
            sp_l.append(spn)
            cs_l.append(csn)
            ss_l.append(ssn)
        xp = xp + rms_norm(op, norm_post[i])
        xs = xs + rms_norm(os_, norm_post[i])
    return (xp, xs, jnp.stack(kp_l), jnp.stack(vp_l), jnp.stack(ks_l), jnp.stack(vs_l),
            jnp.stack(cp_l), jnp.stack(sp_l), jnp.stack(cs_l), jnp.stack(ss_l))
```

```python
import functools
import math

import numpy as np
import jax
import jax.numpy as jnp
from jax import lax
from jax.experimental import pallas as pl
from jax.experimental.pallas import tpu as pltpu

ATTN_HEAD_DIM = 128
N_BUCKETS = 32
MAX_DISTANCE = 128
SUBLN_EPS = 1e-5
NORM_EPS = 1e-6
SSM_HEAD_DIM = 64
SSM_GROUPS = 8
SSM_STATE = 128
CONV_WIDTH = 4
SSD_CHUNK = 128
N_MIXERS = 2

V7X_VMEM_LIMIT_BYTES = 56 * 1024 * 1024
LANES = 128

F32 = jnp.float32
BF16 = jnp.bfloat16
NEG_INF = float("-inf")


def _cparams(*sem):
    return pltpu.CompilerParams(dimension_semantics=sem, vmem_limit_bytes=V7X_VMEM_LIMIT_BYTES)


def _bucket_thresholds():
    n = np.arange(0, 4 * MAX_DISTANCE)
    max_exact = N_BUCKETS // 2
    nf = np.maximum(n, max_exact).astype(np.float32)
    large = max_exact + (np.log(nf / max_exact) / math.log(MAX_DISTANCE / max_exact)
                         * (N_BUCKETS - max_exact)).astype(np.int32)
    bucket = np.where(n < max_exact, n, np.minimum(large, N_BUCKETS - 1))
    return [int(np.argmax(bucket >= b)) for b in range(N_BUCKETS)]


BUCKET_THR = _bucket_thresholds()
FAR_DISTANCE = BUCKET_THR[N_BUCKETS - 1]


def _bias_of_distance(n, rel_of_bucket):
    v = jnp.broadcast_to(rel_of_bucket(0), n.shape).astype(F32)
    for b in range(1, N_BUCKETS):
        v = jnp.where(n >= BUCKET_THR[b], rel_of_bucket(b), v)
    return v


def _bias_tiles_kernel(rel_ref, out_ref, *, tile):
    h = pl.program_id(0)
    i = lax.broadcasted_iota(jnp.int32, (tile, tile), 0)
    j = lax.broadcasted_iota(jnp.int32, (tile, tile), 1)
    d = i - j
    rel = lambda b: rel_ref[b, h]
    out_ref[0, 0] = jnp.where(d >= 0, _bias_of_distance(d, rel), NEG_INF)
    out_ref[0, 1] = _bias_of_distance(d + tile, rel)


def _bias_tiles(rel_bias, tile):
    n_heads = rel_bias.shape[1]
    return pl.pallas_call(
        functools.partial(_bias_tiles_kernel, tile=tile),
        grid=(n_heads,),
        in_specs=[pl.BlockSpec(memory_space=pltpu.SMEM)],
        out_specs=pl.BlockSpec((1, 2, tile, tile), lambda h: (h, 0, 0, 0)),
        out_shape=jax.ShapeDtypeStruct((n_heads, 2, tile, tile), F32),
        compiler_params=_cparams("arbitrary"),
    )(rel_bias)


def _sample_prep_kernel(relc_ref, lq_ref, bias_last_ref, bias_new_ref, far_ref, lam_ref, *, page, n_new, lam_init):
    ncol = relc_ref.shape[1]
    key = lax.broadcasted_iota(jnp.int32, (page, ncol), 0)
    col = lax.broadcasted_iota(jnp.int32, (page, ncol), 1)
    qi = col % n_new
    rel = lambda b: relc_ref[b:b + 1, :]
    bias_last_ref[...] = _bias_of_distance(page + qi - key, rel)
    d = qi - key
    bias_new_ref[...] = jnp.where(d >= 0, _bias_of_distance(d, rel), NEG_INF)
    far_ref[...] = relc_ref[N_BUCKETS - 1:N_BUCKETS, :]
    lq = lq_ref[...]
    lam = (jnp.exp(jnp.sum(lq[0:1] * lq[1:2], axis=-1, keepdims=True))
           - jnp.exp(jnp.sum(lq[2:3] * lq[3:4], axis=-1, keepdims=True)) + lam_init)
    lam_ref[...] = lam


def _sample_prep(rel_bias, lambda_qk_l, page, n_new, lam_init):
    n_heads = rel_bias.shape[1]
    ncol = 2 * n_heads * n_new
    relc = jnp.tile(jnp.repeat(rel_bias, n_new, axis=1), (1, 2))
    full = lambda shape: pl.BlockSpec(shape, lambda: (0,) * len(shape))
    return pl.pallas_call(
        functools.partial(_sample_prep_kernel, page=page, n_new=n_new, lam_init=lam_init),
        in_specs=[full(relc.shape), full(lambda_qk_l.shape)],
        out_specs=[full((page, ncol)), full((page, ncol)), full((1, ncol)), full((1, 1))],
        out_shape=[jax.ShapeDtypeStruct((page, ncol), F32), jax.ShapeDtypeStruct((page, ncol), F32),
                   jax.ShapeDtypeStruct((1, ncol), F32), jax.ShapeDtypeStruct((1, 1), F32)],
    )(relc, lambda_qk_l)


def _rmsnorm_kernel(x_ref, w_ref, o_ref, *, eps):
    x = x_ref[...]
    y = x * lax.rsqrt(jnp.mean(x * x, axis=-1, keepdims=True) + eps)
    o_ref[...] = (y * w_ref[...]).astype(o_ref.dtype)


def _rmsnorm(x2d, w, eps, out_dtype):
    m, d = x2d.shape
    tm = min(m, 256)
    return pl.pallas_call(
        functools.partial(_rmsnorm_kernel, eps=eps),
        grid=(m // tm,),
        in_specs=[pl.BlockSpec((tm, d), lambda i: (i, 0)), pl.BlockSpec((1, d), lambda i: (0, 0))],
        out_specs=pl.BlockSpec((tm, d), lambda i: (i, 0)),
        out_shape=jax.ShapeDtypeStruct((m, d), out_dtype),
        compiler_params=_cparams("parallel"),
    )(x2d, w.reshape(1, d))


def _residual_norm_kernel(x_ref, y_ref, w_ref, o_ref, *, eps):
    y = y_ref[...]
    yn = y * lax.rsqrt(jnp.mean(y * y, axis=-1, keepdims=True) + eps)
    o_ref[...] = x_ref[...] + yn * w_ref[...]


def _residual_norm(x2d, y2d, w, eps):
    m, d = x2d.shape
    tm = min(m, 256)
    row = pl.BlockSpec((tm, d), lambda i: (i, 0))
    return pl.pallas_call(
        functools.partial(_residual_norm_kernel, eps=eps),
        grid=(m // tm,),
        in_specs=[row, row, pl.BlockSpec((1, d), lambda i: (0, 0))],
        out_specs=row,
        out_shape=jax.ShapeDtypeStruct((m, d), F32),
        compiler_params=_cparams("parallel"),
    )(x2d, y2d, w.reshape(1, d))


def _matmul_kernel(a_ref, w_ref, *o_refs):
    acc = jnp.dot(a_ref[...], w_ref[...], preferred_element_type=F32)
    for o_ref in o_refs:
        o_ref[...] = acc.astype(o_ref.dtype)


def _matmul(a, w, col0, n, out_dtypes, tm, tn):
    m, k = a.shape
    tm = min(tm, m)
    tn = min(tn, n)
    assert m % tm == 0 and n % tn == 0 and col0 % tn == 0
    cb0 = col0 // tn
    outs = pl.pallas_call(
        _matmul_kernel,
        grid=(n // tn, m // tm),
        in_specs=[pl.BlockSpec((tm, k), lambda j, i: (i, 0)),
                  pl.BlockSpec((k, tn), lambda j, i: (0, cb0 + j))],
        out_specs=[pl.BlockSpec((tm, tn), lambda j, i: (i, j)) for _ in out_dtypes],
        out_shape=[jax.ShapeDtypeStruct((m, n), dt) for dt in out_dtypes],
        compiler_params=_cparams("parallel", "parallel"),
    )(a, w)
    return outs


def _silu(g):
    return g * (1.0 / (1.0 + jnp.exp(-g)))


def _subln_gate(o, g, sw, lam_init):
    o = o * lax.rsqrt(jnp.mean(o * o, axis=-1, keepdims=True) + SUBLN_EPS) * sw
    return o * (1.0 - lam_init) * _silu(g)


def _prompt_attn_kernel(lam_ref, rel_ref, q_ref, k_ref, v_ref, g_ref, bias_ref, sw_ref, o_ref,
                        m_ref, l_ref, acc_ref, *, tile, scale, lam_init):
    h = pl.program_id(1)
    qb = pl.program_id(2)
    d = ATTN_HEAD_DIM
    far = rel_ref[N_BUCKETS - 1, h]

    def scores(j, kblk, bias):
        s = lax.dot_general(q_ref[:, j * d:(j + 1) * d], kblk[:, j * d:(j + 1) * d],
                            (((1,), (1,)), ((), ())), preferred_element_type=F32)
        return s * scale + bias

    def first_block(row0, bias):
        kblk = k_ref[pl.ds(row0, tile), :]
        vblk = v_ref[pl.ds(row0, tile), :]
        for j in range(2):
            s = scores(j, kblk, bias)
            m = jnp.max(s, axis=-1, keepdims=True)
            p = jnp.exp(s - m)
            m_ref[j] = m
            l_ref[j] = jnp.sum(p, axis=-1, keepdims=True)
            acc_ref[j] = jnp.dot(p.astype(BF16), vblk, preferred_element_type=F32)

    def next_block(row0, bias):
        kblk = k_ref[pl.ds(row0, tile), :]
        vblk = v_ref[pl.ds(row0, tile), :]
        for j in range(2):
            s = scores(j, kblk, bias)
            m_old = m_ref[j]
            m = jnp.maximum(m_old, jnp.max(s, axis=-1, keepdims=True))
            alpha = jnp.exp(m_old - m)
            p = jnp.exp(s - m)
            m_ref[j] = m
            l_ref[j] = alpha * l_ref[j] + jnp.sum(p, axis=-1, keepdims=True)
            acc_ref[j] = alpha * acc_ref[j] + jnp.dot(p.astype(BF16), vblk, preferred_element_type=F32)

    first_block(pl.multiple_of(qb * tile, tile), bias_ref[0, 0])

    @pl.when(qb >= 1)
    def _():
        next_block(pl.multiple_of((qb - 1) * tile, tile), bias_ref[0, 1])

    def far_body(kb, carry):
        next_block(pl.multiple_of(kb * tile, tile), far)
        return carry

    lax.fori_loop(0, jnp.maximum(qb - 1, 0), far_body, 0)

    lam = lam_ref[0, 0]
    o = acc_ref[0] / l_ref[0] - lam * (acc_ref[1] / l_ref[1])
    o_ref[...] = _subln_gate(o, g_ref[...], sw_ref[...], lam_init).astype(o_ref.dtype)


def _prompt_attention(lam, rel_bias, q, k, v, g, bias_tiles, subln_w, batch, seq, tile, lam_init):
    m, width = q.shape
    hw = 2 * ATTN_HEAD_DIM
    n_heads = width // hw
    nq = seq // tile
    smem = pl.BlockSpec(memory_space=pltpu.SMEM)
    qspec = pl.BlockSpec((tile, hw), lambda b, h, i: (b * nq + i, h))
    kvspec = pl.BlockSpec((seq, hw), lambda b, h, i: (b, h))
    return pl.pallas_call(
        functools.partial(_prompt_attn_kernel, tile=tile, scale=ATTN_HEAD_DIM ** -0.5, lam_init=lam_init),
        grid=(batch, n_heads, nq),
        in_specs=[smem, smem, qspec, kvspec, kvspec, qspec,
                  pl.BlockSpec((1, 2, tile, tile), lambda b, h, i: (h, 0, 0, 0)),
                  pl.BlockSpec((1, hw), lambda b, h, i: (0, 0))],
        out_specs=qspec,
        out_shape=jax.ShapeDtypeStruct((m, width), BF16),
        scratch_shapes=[pltpu.VMEM((2, tile, 1), F32), pltpu.VMEM((2, tile, 1), F32),
                        pltpu.VMEM((2, tile, hw), F32)],
        compiler_params=_cparams("parallel", "parallel", "arbitrary"),
    )(lam, rel_bias, q, k, v, g, bias_tiles, subln_w.reshape(1, hw))


def _sample_attn_kernel(pt_ref, lam_ref, qbd_ref, ck_ref, cv_ref, kn_ref, vn_ref, g_ref,
                        bias_last_ref, bias_new_ref, far_ref, sw_ref, o_ref,
                        s_ref, m_ref, w_ref, acc_ref, *, n_pages, n_new, scale, lam_init):
    t = pl.program_id(1)
    page = ck_ref.shape[1]
    half = s_ref.shape[2] // 2
    hw = 2 * ATTN_HEAD_DIM
    n_heads = acc_ref.shape[1] // hw

    def score_page(kpage, bias, slot):
        s = jnp.dot(kpage.astype(BF16), qbd_ref[0], preferred_element_type=F32) * scale + bias
        s_ref[slot] = s
        return jnp.max(s, axis=0, keepdims=True)

    @pl.when(t < n_pages)
    def _():
        bias = jnp.where(t == n_pages - 1, bias_last_ref[...], far_ref[...])
        pmax = score_page(ck_ref[0], bias, t)

        @pl.when(t == 0)
        def _():
            m_ref[...] = pmax

        @pl.when(t > 0)
        def _():
            m_ref[...] = jnp.maximum(m_ref[...], pmax)

    @pl.when(t == n_pages - 1)
    def _():
        m = jnp.maximum(m_ref[...], score_page(kn_ref[0], bias_new_ref[...], n_pages))

        def body(slot, l):
            p = jnp.exp(s_ref[slot] - m)
            s_ref[slot] = p
            return l + jnp.sum(p, axis=0, keepdims=True)

        l = lax.fori_loop(0, n_pages + 1, body, jnp.zeros_like(m))
        inv = 1.0 / l
        w_ref[0:1, :] = inv[:, :half]
        w_ref[1:2, :] = lam_ref[0, 0] * inv[:, half:]
        acc_ref[...] = jnp.zeros_like(acc_ref)

    def value_page(vpage_ref, slot):
        p = s_ref[slot]
        a = p[:, :half] * w_ref[0:1, :] - p[:, half:] * w_ref[1:2, :]
        at = a.T.astype(BF16)
        rows = 2 * n_new
        for h in range(n_heads):
            r0 = (h * n_new) // rows * rows
            res = jnp.dot(at[r0:r0 + rows, :], vpage_ref[0, :, h * hw:(h + 1) * hw].astype(BF16),
                          preferred_element_type=F32)
            off = h * n_new - r0
            acc_ref[:, h * hw:(h + 1) * hw] += res[off:off + n_new, :]

    @pl.when(t >= n_pages)
    def _():
        value_page(cv_ref, t - n_pages)

    @pl.when(t == 2 * n_pages - 1)
    def _():
        value_page(vn_ref, n_pages)
        for h in range(n_heads):
            cols = slice(h * hw, (h + 1) * hw)
            o_ref[0, :, cols] = _subln_gate(acc_ref[:, cols], g_ref[0, :, cols], sw_ref[...], lam_init)


def _sample_attention(page_table, lam, qbd, cache_k2, cache_v2, layer_page0, k_new_pad, v_new_pad, g,
                      bias_last, bias_new, far, subln_w, n_new, lam_init):
    bs, n_pages = page_table.shape
    _, page, width = cache_k2.shape
    ncol = qbd.shape[2]
    hw = 2 * ATTN_HEAD_DIM
    assert ncol % 2 == 0 and page + 1 >= FAR_DISTANCE

    def kmap(b, t, pt):
        return (layer_page0 + pt[b, jnp.minimum(t, n_pages - 1)], 0, 0)

    def vmap(b, t, pt):
        return (layer_page0 + pt[b, jnp.maximum(t - n_pages, 0)], 0, 0)

    per_seq = lambda shape: pl.BlockSpec((1,) + shape, lambda b, t, pt: (b, 0, 0))
    const = lambda shape: pl.BlockSpec(shape, lambda b, t, pt: (0, 0))
    grid_spec = pltpu.PrefetchScalarGridSpec(
        num_scalar_prefetch=1,
        grid=(bs, 2 * n_pages),
        in_specs=[pl.BlockSpec(memory_space=pltpu.SMEM),
                  per_seq((width, ncol)),
                  pl.BlockSpec((1, page, width), kmap),
                  pl.BlockSpec((1, page, width), vmap),
                  per_seq((page, width)), per_seq((page, width)), per_seq((n_new, width)),
                  const((page, ncol)), const((page, ncol)), const((1, ncol)), const((1, hw))],
        out_specs=per_seq((n_new, width)),
        scratch_shapes=[pltpu.VMEM((n_pages + 1, page, ncol), F32), pltpu.VMEM((1, ncol), F32),
                        pltpu.VMEM((2, ncol // 2), F32), pltpu.VMEM((n_new, width), F32)],
    )
    return pl.pallas_call(
        functools.partial(_sample_attn_kernel, n_pages=n_pages, n_new=n_new,
                          scale=ATTN_HEAD_DIM ** -0.5, lam_init=lam_init),
        grid_spec=grid_spec,
        out_shape=jax.ShapeDtypeStruct((bs, n_new, width), F32),
        compiler_params=_cparams("parallel", "arbitrary"),
    )(page_table, lam, qbd, cache_k2, cache_v2, k_new_pad, v_new_pad, g,
      bias_last, bias_new, far, subln_w.reshape(1, hw))


def _block_diag_queries(q, bs, n_new):
    d = ATTN_HEAD_DIM
    n_maps = q.shape[1] // d
    n_heads = n_maps // 2
    q5 = q.reshape(bs, n_new, n_heads, 2, d)
    qt = jnp.transpose(q5, (0, 2, 3, 4, 1))
    eye_h = jnp.eye(n_heads, dtype=q.dtype)
    eye_j = jnp.eye(2, dtype=q.dtype)
    out = (qt[:, :, :, :, None, None, :] * eye_j[None, None, :, None, :, None, None]
           * eye_h[None, :, None, None, None, :, None])
    return out.reshape(bs, n_maps * d, 2 * n_heads * n_new)


def _conv_kernel(x_ref, st_ref, w_ref, b_ref, y_ref, tail_ref, pad_ref, *, width):
    t = pl.program_id(2)
    tt = x_ref.shape[0]
    lo = 8 - (width - 1)

    @pl.when(t == 0)
    def _():
        pad_ref[lo:8, :] = st_ref[0]

    x = x_ref[...]
    pad_ref[8:8 + tt, :] = x
    acc = b_ref[...] + x * w_ref[width - 1:width, :]
    for j in range(width - 1):
        acc = acc + pad_ref[lo + j:lo + j + tt, :] * w_ref[j:j + 1, :]
    y_ref[...] = _silu(acc)
    tail = x[tt - (width - 1):, :]
    pad_ref[lo:8, :] = tail
    tail_ref[0] = tail


def _conv(xbc, conv_state, conv_w, conv_b, batch, seq):
    m, c = xbc.shape
    width = conv_w.shape[0]
    tt = min(seq, 256)
    tc = 1024
    assert seq % tt == 0 and c % tc == 0 and tt >= width - 1
    nt = seq // tt
    return pl.pallas_call(
        functools.partial(_conv_kernel, width=width),
        grid=(batch, c // tc, nt),
        in_specs=[pl.BlockSpec((tt, tc), lambda b, j, t: (b * nt + t, j)),
                  pl.BlockSpec((1, width - 1, tc), lambda b, j, t: (b, 0, j)),
                  pl.BlockSpec((width, tc), lambda b, j, t: (0, j)),
                  pl.BlockSpec((1, tc), lambda b, j, t: (0, j))],
        out_specs=[pl.BlockSpec((tt, tc), lambda b, j, t: (b * nt + t, j)),
                   pl.BlockSpec((1, width - 1, tc), lambda b, j, t: (b, 0, j))],
        out_shape=[jax.ShapeDtypeStruct((m, c), F32),
                   jax.ShapeDtypeStruct((batch, width - 1, c), F32)],
        scratch_shapes=[pltpu.VMEM((8 + tt, tc), F32)],
        compiler_params=_cparams("parallel", "parallel", "arbitrary"),
    )(xbc, conv_state, conv_w, conv_b.reshape(1, c))


def _dt_kernel(x_ref, bias_ref, alog_ref, dt_ref, cum_ref, dec_ref):
    q = x_ref.shape[0]
    x = x_ref[...] + bias_ref[...]
    dt = jnp.maximum(x, 0.0) + jnp.log1p(jnp.exp(-jnp.abs(x)))
    c = dt * (-jnp.exp(alog_ref[...]))
    row = lax.broadcasted_iota(jnp.int32, c.shape, 0)
    shift = 1
    while shift < q:
        c = c + jnp.where(row >= shift, pltpu.roll(c, shift, axis=0), 0.0)
        shift *= 2
    dt_ref[...] = dt
    cum_ref[...] = c
    dec_ref[0] = jnp.exp(c[q - 1:q, :])


def _dt_prep(dt_raw, dt_bias, a_log, q):
    m, nh = dt_raw.shape
    row = pl.BlockSpec((q, nh), lambda i: (i, 0))
    vec = pl.BlockSpec((1, nh), lambda i: (0, 0))
    return pl.pallas_call(
        _dt_kernel,
        grid=(m // q,),
        in_specs=[row, vec, vec],
        out_specs=[row, row, pl.BlockSpec((1, 1, nh), lambda i: (i, 0, 0))],
        out_shape=[jax.ShapeDtypeStruct((m, nh), F32), jax.ShapeDtypeStruct((m, nh), F32),
                   jax.ShapeDtypeStruct((m // q, 1, nh), F32)],
        compiler_params=_cparams("parallel"),
    )(dt_raw, dt_bias.reshape(1, nh), a_log.reshape(1, nh))


def _ssd_kernel(*refs, has_init, n_chunks):
    if has_init:
        (x_ref, b_ref, c_ref, z_ref, dt_ref, cum_ref, dtt_ref, cumt_ref, dec_ref, dskip_ref, gw_ref, s0_ref,
         y_ref, sout_ref, state_ref, ys_ref, xw_ref) = refs
    else:
        (x_ref, b_ref, c_ref, z_ref, dt_ref, cum_ref, dtt_ref, cumt_ref, dec_ref, dskip_ref, gw_ref,
         y_ref, sout_ref, state_ref, ys_ref, xw_ref) = refs
    ck = pl.program_id(2)
    q = x_ref.shape[0]
    r_heads = dt_ref.shape[3]
    p = SSM_HEAD_DIM

    @pl.when(ck == 0)
    def _():
        if has_init:
            state_ref[...] = s0_ref[0, 0].T
        else:
            state_ref[...] = jnp.zeros_like(state_ref)

    x = x_ref[...]
    xb = x.astype(BF16)
    bb = b_ref[...].astype(BF16)
    cb_ = c_ref[...].astype(BF16)
    cb = lax.dot_general(cb_, bb, (((1,), (1,)), ((), ())), preferred_element_type=F32)
    cs = jnp.dot(cb_, state_ref[...].astype(BF16), preferred_element_type=F32)
    dt = dt_ref[0, 0]
    cum = cum_ref[0, 0]
    dtt = dtt_ref[0, 0]
    cumt = cumt_ref[0, 0]
    w_end = jnp.exp(cum[q - 1:q, :] - cum) * dt
    ecum = jnp.exp(cum)
    li = lax.broadcasted_iota(jnp.int32, (q, q), 0)
    si = lax.broadcasted_iota(jnp.int32, (q, q), 1)
    causal = li >= si
    for r in range(r_heads):
        cols = slice(r * p, (r + 1) * p)
        seg = cum[:, r:r + 1] - cumt[r:r + 1, :]
        decay = jnp.exp(jnp.where(causal, seg, NEG_INF))
        mh = (cb * decay * dtt[r:r + 1, :]).astype(BF16)
        ys_ref[:, cols] = (jnp.dot(mh, xb[:, cols], preferred_element_type=F32)
                           + cs[:, cols] * ecum[:, r:r + 1])
        xw_ref[:, cols] = (x[:, cols] * w_end[:, r:r + 1]).astype(BF16)
    ds = lax.dot_general(bb, xw_ref[...], (((0,), (0,)), ((), ())), preferred_element_type=F32)
    state_ref[...] = state_ref[...] * dec_ref[0, 0, 0] + ds

    y = ys_ref[...] + x * dskip_ref[...]
    y = y * _silu(z_ref[...])
    y = y * lax.rsqrt(jnp.mean(y * y, axis=-1, keepdims=True) + NORM_EPS) * gw_ref[...]
    y_ref[...] = y.astype(y_ref.dtype)

    @pl.when(ck == n_chunks - 1)
    def _():
        sout_ref[0, 0] = state_ref[...].T


def _ssd_scan(act, z, dt, cum, dec_last, d_skip, gnorm_w, state0, batch, seq, q, out_dtype):
    m = act.shape[0]
    inner = z.shape[1]
    g = SSM_GROUPS
    n = SSM_STATE
    gw_cols = inner // g
    r = gw_cols // SSM_HEAD_DIM
    nc = seq // q
    assert n == LANES and gw_cols % LANES == 0
    dt4 = jnp.transpose(dt.reshape(batch, seq, g, r), (0, 2, 1, 3))
    cum4 = jnp.transpose(cum.reshape(batch, seq, g, r), (0, 2, 1, 3))
    dtt4 = jnp.transpose(dt4, (0, 1, 3, 2))
    cumt4 = jnp.transpose(cum4, (0, 1, 3, 2))
    dec5 = jnp.repeat(dec_last.reshape(batch, nc, g, 1, r), SSM_HEAD_DIM, axis=-1)
    dskip_row = jnp.repeat(d_skip, SSM_HEAD_DIM).reshape(1, inner)

    xcol0 = inner // gw_cols
    row_blk = lambda off: pl.BlockSpec((q, gw_cols), lambda b, gi, c: (b * nc + c, off + gi))
    bspec = pl.BlockSpec((q, n), lambda b, gi, c: (b * nc + c, inner // n + gi))
    cspec = pl.BlockSpec((q, n), lambda b, gi, c: (b * nc + c, inner // n + g + gi))
    tcol = pl.BlockSpec((1, 1, q, r), lambda b, gi, c: (b, gi, c, 0))
    trow = pl.BlockSpec((1, 1, r, q), lambda b, gi, c: (b, gi, 0, c))
    vec = pl.BlockSpec((1, gw_cols), lambda b, gi, c: (0, gi))
    sspec = pl.BlockSpec((1, 1, gw_cols, n), lambda b, gi, c: (b, gi, 0, 0))
    in_specs = [row_blk(0), bspec, cspec, row_blk(0), tcol, tcol, trow, trow,
                pl.BlockSpec((1, 1, 1, 1, gw_cols), lambda b, gi, c: (b, c, gi, 0, 0)), vec, vec]
    args = [act, act, act, z, dt4, cum4, dtt4, cumt4, dec5, dskip_row, gnorm_w.reshape(1, inner)]
    if state0 is not None:
        in_specs.append(sspec)
        args.append(state0.reshape(batch, g, gw_cols, n))
    del xcol0
    y, s_out = pl.pallas_call(
        functools.partial(_ssd_kernel, has_init=state0 is not None, n_chunks=nc),
        grid=(batch, g, nc),
        in_specs=in_specs,
        out_specs=[row_blk(0), sspec],
        out_shape=[jax.ShapeDtypeStruct((m, inner), out_dtype),
                   jax.ShapeDtypeStruct((batch, g, gw_cols, n), F32)],
        scratch_shapes=[pltpu.VMEM((n, gw_cols), F32), pltpu.VMEM((q, gw_cols), F32),
                        pltpu.VMEM((q, gw_cols), BF16)],
        compiler_params=_cparams("parallel", "parallel", "arbitrary"),
    )(*args)
    return y, s_out


def _attn_layer(hp, hs, w_in, w_out, lambda_qk_l, subln_w_l, rel_bias, cache_k, cache_v, la, page_table,
                batch, seq, bs, n_new, lam_init):
    width = w_out.shape[0]
    tile = min(seq, 512)
    assert seq % tile == 0 and tile + 1 >= FAR_DISTANCE
    page = cache_k.shape[2]
    n_phys = cache_k.shape[1]

    bias_tiles = _bias_tiles(rel_bias, tile)
    bias_last, bias_new, far, lam = _sample_prep(rel_bias, lambda_qk_l, page, n_new, lam_init)

    (q,) = _matmul(hp, w_in, 0, width, [BF16], 512, 1024)
    k, kb = _matmul(hp, w_in, width, width, [F32, BF16], 512, 1024)
    v, vb = _matmul(hp, w_in, 2 * width, width, [F32, BF16], 512, 1024)
    (g,) = _matmul(hp, w_in, 3 * width, width, [F32], 512, 1024)
    og = _prompt_attention(lam, rel_bias, q, kb, vb, g, bias_tiles, subln_w_l, batch, seq, tile, lam_init)
    (op,) = _matmul(og, w_out, 0, w_out.shape[1], [F32], 512, 1024)

    (q2,) = _matmul(hs, w_in, 0, width, [BF16], 512, 1024)
    (k2,) = _matmul(hs, w_in, width, width, [F32], 512, 1024)
    (v2,) = _matmul(hs, w_in, 2 * width, width, [F32], 512, 1024)
    (g2,) = _matmul(hs, w_in, 3 * width, width, [F32], 512, 1024)
    qbd = _block_diag_queries(q2, bs, n_new)
    pad_rows = lambda a: jnp.pad(a.reshape(bs, n_new, width), ((0, 0), (0, page - n_new), (0, 0)))
    og2 = _sample_attention(page_table, lam, qbd,
                            cache_k.reshape(-1, page, width), cache_v.reshape(-1, page, width), la * n_phys,
                            pad_rows(k2), pad_rows(v2), g2.reshape(bs, n_new, width),
                            bias_last, bias_new, far, subln_w_l, n_new, lam_init)
    (os_,) = _matmul(og2.reshape(bs * n_new, width).astype(BF16), w_out, 0, w_out.shape[1], [F32], 512, 1024)
    return op, os_, k, v, k2, v2


def _ssd_layer(h, conv_state, ssm_state, w_in, conv_w, conv_b, dt_bias, a_log, d_skip, gnorm_w, w_out,
               batch, seq):
    inner = w_out.shape[0]
    conv_dim = conv_w.shape[1]
    n_heads = dt_bias.shape[0]
    q = SSD_CHUNK if seq % SSD_CHUNK == 0 else seq
    (z,) = _matmul(h, w_in, 0, inner, [F32], 512, 1024)
    (xbc,) = _matmul(h, w_in, inner, conv_dim, [F32], 512, 1024)
    (dt_raw,) = _matmul(h, w_in, inner + conv_dim, n_heads, [F32], 512, n_heads)
    if conv_state is None:
        conv_state = jnp.zeros((batch, conv_w.shape[0] - 1, conv_dim), F32)
    act, conv_out = _conv(xbc, conv_state, conv_w, conv_b, batch, seq)
    dt, cum, dec_last = _dt_prep(dt_raw, dt_bias, a_log, q)
    out_dtype = BF16 if q % 16 == 0 else F32
    y, s_out = _ssd_scan(act, z, dt, cum, dec_last, d_skip, gnorm_w, ssm_state, batch, seq, q, out_dtype)
    (o,) = _matmul(y.astype(BF16), w_out, 0, w_out.shape[1], [F32], 512, 512)
    return o, conv_out, s_out.reshape(batch, n_heads, SSM_HEAD_DIM, SSM_STATE)


def kernel(x_prompt, x_sample, cache_k, cache_v, page_table, state_conv, state_ssm, norm_pre, norm_post, rel_bias,
           w_attn_in, lambda_qk, subln_w, w_attn_out, w_ssm_in, conv_w, conv_b, dt_bias, a_log, d_skip, gnorm_w,
           w_ssm_out):
    bp, sp, dm = x_prompt.shape
    bs, ss, _ = x_sample.shape
    depth = norm_pre.shape[0]
    xp = x_prompt.reshape(bp * sp, dm)
    xs = x_sample.reshape(bs * ss, dm)
    kp_l, vp_l, ks_l, vs_l, cp_l, sp_l, cs_l, ss_l = [], [], [], [], [], [], [], []
    for i in range(depth):
        hp = _rmsnorm(xp, norm_pre[i], NORM_EPS, BF16)
        hs = _rmsnorm(xs, norm_pre[i], NORM_EPS, BF16)
        if i % N_MIXERS == 0:
            la = i // N_MIXERS
            lam_init = 0.8 - 0.6 * math.exp(-0.3 * i)
            op, os_, k, v, k2, v2 = _attn_layer(
                hp, hs, w_attn_in[la].astype(BF16), w_attn_out[la].astype(BF16), lambda_qk[la], subln_w[la],
                rel_bias, cache_k, cache_v, la, page_table, bp, sp, bs, ss, lam_init)
            n_maps = k.shape[1] // ATTN_HEAD_DIM
            kp_l.append(k.reshape(bp, sp, n_maps, ATTN_HEAD_DIM))
            vp_l.append(v.reshape(bp, sp, n_maps // 2, 2 * ATTN_HEAD_DIM))
            ks_l.append(k2.reshape(bs, ss, n_maps, ATTN_HEAD_DIM))
            vs_l.append(v2.reshape(bs, ss, n_maps // 2, 2 * ATTN_HEAD_DIM))
        else:
            ls = i // N_MIXERS
            wts = (w_ssm_in[ls].astype(BF16), conv_w[ls], conv_b[ls], dt_bias[ls], a_log[ls], d_skip[ls],
                   gnorm_w[ls], w_ssm_out[ls].astype(BF16))
            op, cpn, spn = _ssd_layer(hp, None, None, *wts, bp, sp)
            os_, csn, ssn = _ssd_layer(hs, state_conv[ls], state_ssm[ls], *wts, bs, ss)
            cp_l.append(cpn)
            sp_l.append(spn)
            cs_l.append(csn)
            ss_l.append(ssn)
        xp = _residual_norm(xp, op, norm_post[i], NORM_EPS)
        xs = _residual_norm(xs, os_, norm_post[i], NORM_EPS)
    return (xp.reshape(bp, sp, dm), xs.reshape(bs, ss, dm),
            jnp.stack(kp_l), jnp.stack(vp_l), jnp.stack(ks_l), jnp.stack(vs_l),
            jnp.stack(cp_l), jnp.stack(sp_l), jnp.stack(cs_l), jnp.stack(ss_l))
```

```python
import functools
import math

import numpy as np
import jax
import jax.numpy as jnp
from jax import lax
from jax.experimental import pallas as pl
from jax.experimental.pallas import tpu as pltpu

ATTN_HEAD_DIM = 128
N_BUCKETS = 32
MAX_DISTANCE = 128
SUBLN_EPS = 1e-5
NORM_EPS = 1e-6
SSM_HEAD_DIM = 64
SSM_GROUPS = 8
SSM_STATE = 128
CONV_WIDTH = 4
SSD_CHUNK = 128
N_MIXERS = 2

V7X_VMEM_LIMIT_BYTES = 56 * 1024 * 1024
LANES = 128

F32 = jnp.float32
BF16 = jnp.bfloat16
NEG_INF = float("-inf")


def _cparams(*sem):
    return pltpu.CompilerParams(dimension_semantics=sem, vmem_limit_bytes=V7X_VMEM_LIMIT_BYTES)


def _bucket_thresholds():
    n = np.arange(0, 4 * MAX_DISTANCE)
    max_exact = N_BUCKETS // 2
    nf = np.maximum(n, max_exact).astype(np.float32)
    large = max_exact + (np.log(nf / max_exact) / math.log(MAX_DISTANCE / max_exact)
                         * (N_BUCKETS - max_exact)).astype(np.int32)
    bucket = np.where(n < max_exact, n, np.minimum(large, N_BUCKETS - 1))
    return [int(np.argmax(bucket >= b)) for b in range(N_BUCKETS)]


BUCKET_THR = _bucket_thresholds()
FAR_DISTANCE = BUCKET_THR[N_BUCKETS - 1]


def _bias_of_distance(n, rel_of_bucket):
    v = jnp.broadcast_to(rel_of_bucket(0), n.shape).astype(F32)
    for b in range(1, N_BUCKETS):
        v = jnp.where(n >= BUCKET_THR[b], rel_of_bucket(b), v)
    return v


def _bias_tiles_kernel(rel_ref, out_ref, *, tile):
    h = pl.program_id(0)
    i = lax.broadcasted_iota(jnp.int32, (tile, tile), 0)
    j = lax.broadcasted_iota(jnp.int32, (tile, tile), 1)
    d = i - j
    rel = lambda b: rel_ref[b, h]
    out_ref[0, 0] = jnp.where(d >= 0, _bias_of_distance(d, rel), NEG_INF)
    out_ref[0, 1] = _bias_of_distance(d + tile, rel)


def _bias_tiles(rel_bias, tile):
    n_heads = rel_bias.shape[1]
    return pl.pallas_call(
        functools.partial(_bias_tiles_kernel, tile=tile),
        grid=(n_heads,),
        in_specs=[pl.BlockSpec(memory_space=pltpu.SMEM)],
        out_specs=pl.BlockSpec((1, 2, tile, tile), lambda h: (h, 0, 0, 0)),
        out_shape=jax.ShapeDtypeStruct((n_heads, 2, tile, tile), F32),
        compiler_params=_cparams("arbitrary"),
        name="bias_tiles",
    )(rel_bias)


def _sample_prep_kernel(relc_ref, lq_ref, bias_last_ref, bias_new_ref, far_ref, lam_ref, *, page, n_new, lam_init):
    ncol = relc_ref.shape[1]
    key = lax.broadcasted_iota(jnp.int32, (page, ncol), 0)
    col = lax.broadcasted_iota(jnp.int32, (page, ncol), 1)
    qi = col % n_new
    rel = lambda b: relc_ref[b:b + 1, :]
    bias_last_ref[...] = _bias_of_distance(page + qi - key, rel)
    d = qi - key
    bias_new_ref[...] = jnp.where(d >= 0, _bias_of_distance(d, rel), NEG_INF)
    far_ref[...] = relc_ref[N_BUCKETS - 1:N_BUCKETS, :]
    lq = lq_ref[...]
    lam = (jnp.exp(jnp.sum(lq[0:1] * lq[1:2], axis=-1, keepdims=True))
           - jnp.exp(jnp.sum(lq[2:3] * lq[3:4], axis=-1, keepdims=True)) + lam_init)
    lam_ref[...] = lam


def _sample_prep(rel_bias, lambda_qk_l, page, n_new, lam_init):
    n_heads = rel_bias.shape[1]
    ncol = 2 * n_heads * n_new
    relc = jnp.tile(jnp.repeat(rel_bias, n_new, axis=1), (1, 2))
    full = lambda shape: pl.BlockSpec(shape, lambda: (0,) * len(shape))
    return pl.pallas_call(
        functools.partial(_sample_prep_kernel, page=page, n_new=n_new, lam_init=lam_init),
        in_specs=[full(relc.shape), full(lambda_qk_l.shape)],
        out_specs=[full((page, ncol)), full((page, ncol)), full((1, ncol)), full((1, 1))],
        out_shape=[jax.ShapeDtypeStruct((page, ncol), F32), jax.ShapeDtypeStruct((page, ncol), F32),
                   jax.ShapeDtypeStruct((1, ncol), F32), jax.ShapeDtypeStruct((1, 1), F32)],
    )(relc, lambda_qk_l)


def _rmsnorm_kernel(x_ref, w_ref, o_ref, *, eps):
    x = x_ref[...]
    y = x * lax.rsqrt(jnp.mean(x * x, axis=-1, keepdims=True) + eps)
    o_ref[...] = (y * w_ref[...]).astype(o_ref.dtype)


def _rmsnorm(x2d, w, eps, out_dtype):
    m, d = x2d.shape
    tm = min(m, 256)
    return pl.pallas_call(
        functools.partial(_rmsnorm_kernel, eps=eps),
        grid=(m // tm,),
        in_specs=[pl.BlockSpec((tm, d), lambda i: (i, 0)), pl.BlockSpec((1, d), lambda i: (0, 0))],
        out_specs=pl.BlockSpec((tm, d), lambda i: (i, 0)),
        out_shape=jax.ShapeDtypeStruct((m, d), out_dtype),
        compiler_params=_cparams("parallel"),
        name="rmsnorm",
    )(x2d, w.reshape(1, d))


def _residual_norm_kernel(x_ref, y_ref, w_ref, o_ref, *, eps):
    y = y_ref[...]
    yn = y * lax.rsqrt(jnp.mean(y * y, axis=-1, keepdims=True) + eps)
    o_ref[...] = x_ref[...] + yn * w_ref[...]


def _residual_norm(x2d, y2d, w, eps):
    m, d = x2d.shape
    tm = min(m, 256)
    row = pl.BlockSpec((tm, d), lambda i: (i, 0))
    return pl.pallas_call(
        functools.partial(_residual_norm_kernel, eps=eps),
        grid=(m // tm,),
        in_specs=[row, row, pl.BlockSpec((1, d), lambda i: (0, 0))],
        out_specs=row,
        out_shape=jax.ShapeDtypeStruct((m, d), F32),
        compiler_params=_cparams("parallel"),
        name="residual_norm",
    )(x2d, y2d, w.reshape(1, d))


def _matmul_kernel(a_ref, w_ref, *o_refs):
    acc = jnp.dot(a_ref[...], w_ref[...], preferred_element_type=F32)
    for o_ref in o_refs:
        o_ref[...] = acc.astype(o_ref.dtype)


def _matmul(a, w, col0, n, out_dtypes, tm, tn):
    m, k = a.shape
    tm = min(tm, m)
    tn = min(tn, n)
    assert m % tm == 0 and n % tn == 0 and col0 % tn == 0
    cb0 = col0 // tn
    outs = pl.pallas_call(
        _matmul_kernel,
        grid=(n // tn, m // tm),
        in_specs=[pl.BlockSpec((tm, k), lambda j, i: (i, 0)),
                  pl.BlockSpec((k, tn), lambda j, i: (0, cb0 + j))],
        out_specs=[pl.BlockSpec((tm, tn), lambda j, i: (i, j)) for _ in out_dtypes],
        out_shape=[jax.ShapeDtypeStruct((m, n), dt) for dt in out_dtypes],
        compiler_params=_cparams("parallel", "parallel"),
        name="matmul",
    )(a, w)
    return outs


def _silu(g):
    return g * (1.0 / (1.0 + jnp.exp(-g)))


def _subln_gate(o, g, sw, lam_init):
    o = o * lax.rsqrt(jnp.mean(o * o, axis=-1, keepdims=True) + SUBLN_EPS) * sw
    return o * (1.0 - lam_init) * _silu(g)


def _prompt_attn_kernel(lam_ref, rel_ref, q_ref, k_ref, v_ref, g_ref, bias_ref, sw_ref, o_ref,
                        m_ref, l_ref, acc_ref, *, tile, scale, lam_init):
    h = pl.program_id(1)
    qb = pl.program_id(2)
    d = ATTN_HEAD_DIM
    far = rel_ref[N_BUCKETS - 1, h]

    def scores(j, kblk, bias):
        s = lax.dot_general(q_ref[:, j * d:(j + 1) * d], kblk[:, j * d:(j + 1) * d],
                            (((1,), (1,)), ((), ())), preferred_element_type=F32)
        return s * scale + bias

    def row_reduce(x, fold, reduce):
        acc = x[:, 0:LANES]
        for i in range(1, x.shape[1] // LANES):
            acc = fold(acc, x[:, i * LANES:(i + 1) * LANES])
        return reduce(acc, axis=-1, keepdims=True)

    def first_block(row0, bias):
        kblk = k_ref[pl.ds(row0, tile), :]
        vblk = v_ref[pl.ds(row0, tile), :]
        for j in range(2):
            s = scores(j, kblk, bias)
            m = row_reduce(s, jnp.maximum, jnp.max)
            p = jnp.exp(s - m)
            m_ref[j] = m
            l_ref[j] = row_reduce(p, jnp.add, jnp.sum)
            acc_ref[j] = jnp.dot(p.astype(BF16), vblk, preferred_element_type=F32)

    def next_block(row0, bias):
        kblk = k_ref[pl.ds(row0, tile), :]
        vblk = v_ref[pl.ds(row0, tile), :]
        for j in range(2):
            s = scores(j, kblk, bias)
            m_old = m_ref[j]
            m = jnp.maximum(m_old, row_reduce(s, jnp.maximum, jnp.max))
            alpha = jnp.exp(m_old - m)
            p = jnp.exp(s - m)
            m_ref[j] = m
            l_ref[j] = alpha * l_ref[j] + row_reduce(p, jnp.add, jnp.sum)
            acc_ref[j] = alpha * acc_ref[j] + jnp.dot(p.astype(BF16), vblk, preferred_element_type=F32)

    first_block(pl.multiple_of(qb * tile, tile), bias_ref[0, 0])

    @pl.when(qb >= 1)
    def _():
        next_block(pl.multiple_of((qb - 1) * tile, tile), bias_ref[0, 1])

    def far_body(kb, carry):
        next_block(pl.multiple_of(kb * tile, tile), far)
        return carry

    lax.fori_loop(0, jnp.maximum(qb - 1, 0), far_body, 0)

    lam = lam_ref[0, 0]
    o = acc_ref[0] / l_ref[0] - lam * (acc_ref[1] / l_ref[1])
    o_ref[...] = _subln_gate(o, g_ref[...], sw_ref[...], lam_init).astype(o_ref.dtype)


def _prompt_attention(lam, rel_bias, q, k, v, g, bias_tiles, subln_w, batch, seq, tile, lam_init):
    m, width = q.shape
    hw = 2 * ATTN_HEAD_DIM
    n_heads = width // hw
    nq = seq // tile
    smem = pl.BlockSpec(memory_space=pltpu.SMEM)
    qspec = pl.BlockSpec((tile, hw), lambda b, h, i: (b * nq + i, h))
    kvspec = pl.BlockSpec((seq, hw), lambda b, h, i: (b, h))
    return pl.pallas_call(
        functools.partial(_prompt_attn_kernel, tile=tile, scale=ATTN_HEAD_DIM ** -0.5, lam_init=lam_init),
        grid=(batch, n_heads, nq),
        in_specs=[smem, smem, qspec, kvspec, kvspec, qspec,
                  pl.BlockSpec((1, 2, tile, tile), lambda b, h, i: (h, 0, 0, 0)),
                  pl.BlockSpec((1, hw), lambda b, h, i: (0, 0))],
        out_specs=qspec,
        out_shape=jax.ShapeDtypeStruct((m, width), BF16),
        scratch_shapes=[pltpu.VMEM((2, tile, 1), F32), pltpu.VMEM((2, tile, 1), F32),
                        pltpu.VMEM((2, tile, hw), F32)],
        compiler_params=_cparams("parallel", "parallel", "arbitrary"),
        name="prompt_attention",
    )(lam, rel_bias, q, k, v, g, bias_tiles, subln_w.reshape(1, hw))


SUBLANES = 8


def _sample_attn_constants(page, n_maps, n_heads, n_new):
    rb = SUBLANES * n_maps
    perm = np.zeros((rb, rb), np.float32)
    for c in range(n_maps):
        for k8 in range(SUBLANES):
            perm[c * SUBLANES + k8, k8 * n_maps + c] = 1.0
    expand = np.zeros((page, page * n_heads), np.float32)
    for key in range(page):
        expand[key, key * n_heads:(key + 1) * n_heads] = 1.0
    row_head = np.arange(n_heads * n_new)[:, None] // n_new
    col_head = np.arange(page * n_heads)[None, :] % n_heads
    mask = (row_head == col_head).astype(np.float32)
    return jnp.asarray(perm, BF16), jnp.asarray(expand, BF16), jnp.asarray(mask, F32)


def _sample_attn_kernel(pt_ref, lam_ref, qbd_ref, ka_ref, kb_ref, va_ref, vb_ref, kn_ref, vn_ref, g_ref,
                        bias_last_ref, bias_new_ref, far_ref, sw_ref, perm_ref, expand_ref, mask_ref, o_ref,
                        s_ref, sn_ref, m_ref, w_ref, acc_ref, kscr_ref, *, n_steps, n_new, scale, lam_init):
    t = pl.program_id(1)
    d = ATTN_HEAD_DIM
    rb = perm_ref.shape[0]
    n_maps = rb // SUBLANES
    page = ka_ref.shape[2] // n_maps
    n_blocks = page // SUBLANES
    half = s_ref.shape[2] // 2
    hw = 2 * d
    n_heads = n_maps // 2
    pad_new = sn_ref.shape[0]

    def gather_keys(k_ref, row0):
        kb16 = k_ref[0, 0].astype(BF16)
        w = jnp.concatenate([kb16[b * rb:(b + 1) * rb] for b in range(n_blocks)], axis=1)
        out = jnp.dot(perm_ref[...], w, preferred_element_type=F32)

        def block_rows(b):
            return jnp.concatenate([out[c * SUBLANES:(c + 1) * SUBLANES, b * d:(b + 1) * d] for c in range(n_maps)],
                                   axis=1)

        for b in range(0, n_blocks, 2):
            r0 = row0 + b * SUBLANES
            kscr_ref[r0:r0 + 2 * SUBLANES, :] = jnp.concatenate([block_rows(b), block_rows(b + 1)],
                                                                 axis=0).astype(BF16)

    @pl.when(t < n_steps)
    def _():
        gather_keys(ka_ref, 0)
        gather_keys(kb_ref, page)
        s = jnp.dot(kscr_ref[...], qbd_ref[0], preferred_element_type=F32) * scale
        s_ref[t, 0:page, :] = s[0:page] + far_ref[...]
        s_ref[t, page:2 * page, :] = s[page:] + jnp.where(t == n_steps - 1, bias_last_ref[...], far_ref[...])
        pmax = jnp.max(s_ref[t], axis=0, keepdims=True)

        @pl.when(t == 0)
        def _():
            m_ref[...] = pmax

        @pl.when(t > 0)
        def _():
            m_ref[...] = jnp.maximum(m_ref[...], pmax)

    @pl.when(t == n_steps - 1)
    def _():
        kn = jnp.concatenate([kn_ref[0], jnp.zeros((pad_new - n_new, kn_ref.shape[2]), F32)], axis=0)
        sn = (jnp.dot(kn.astype(BF16), qbd_ref[0], preferred_element_type=F32) * scale
              + bias_new_ref[0:pad_new, :])
        m = jnp.maximum(m_ref[...], jnp.max(sn, axis=0, keepdims=True))
        pn = jnp.exp(sn - m)
        sn_ref[...] = pn

        def body(slot, l):
            p = jnp.exp(s_ref[slot] - m)
            s_ref[slot] = p
            return l + jnp.sum(p, axis=0, keepdims=True)

        l = lax.fori_loop(0, n_steps, body, jnp.sum(pn, axis=0, keepdims=True))
        inv = 1.0 / l
        w_ref[0:1, :] = inv[:, :half]
        w_ref[1:2, :] = lam_ref[0, 0] * inv[:, half:]
        acc_ref[...] = jnp.zeros_like(acc_ref)

    def combined_probs_t(p):
        a = p[:, :half] * w_ref[0:1, :] - p[:, half:] * w_ref[1:2, :]
        return a.T.astype(BF16)

    def accumulate_page(p, v_ref):
        at = combined_probs_t(p)
        spread = jnp.dot(at, expand_ref[...], preferred_element_type=F32) * mask_ref[...]
        acc_ref[...] += jnp.dot(spread.astype(BF16), v_ref[0, 0].astype(BF16), preferred_element_type=F32)

    @pl.when(t >= n_steps)
    def _():
        p = s_ref[t - n_steps]
        accumulate_page(p[0:page], va_ref)
        accumulate_page(p[page:], vb_ref)

    @pl.when(t == 2 * n_steps - 1)
    def _():
        pn = jnp.concatenate([sn_ref[...], jnp.zeros((page - pad_new, 2 * half), F32)], axis=0)
        at = combined_probs_t(pn)
        rows = 2 * n_new
        for h in range(n_heads):
            vh = jnp.concatenate([vn_ref[0, :, h * hw:(h + 1) * hw], jnp.zeros((page - n_new, hw), F32)], axis=0)
            r0 = (h * n_new) // rows * rows
            res = jnp.dot(at[r0:r0 + rows, :], vh.astype(BF16), preferred_element_type=F32)
            off = h * n_new - r0
            oh = acc_ref[h * n_new:(h + 1) * n_new, :] + res[off:off + n_new, :]
            cols = slice(h * hw, (h + 1) * hw)
            o_ref[0, :, cols] = _subln_gate(oh, g_ref[0, :, cols], sw_ref[...], lam_init)


def _sample_attention(page_table, lam, qbd, cache_k, cache_v, la, k_new, v_new, g,
                      bias_last, bias_new, far, subln_w, n_new, lam_init):
    bs, n_pages = page_table.shape
    n_layers, n_phys, page, n_maps, d = cache_k.shape
    n_heads = n_maps // 2
    width = n_maps * d
    ncol = qbd.shape[2]
    hw = 2 * ATTN_HEAD_DIM
    pad_new = 16
    assert ncol % 2 == 0 and page + 1 >= FAR_DISTANCE and n_pages % 2 == 0 and n_new <= pad_new <= page
    n_steps = n_pages // 2
    perm, expand, mask = _sample_attn_constants(page, n_maps, n_heads, n_new)
    ck = cache_k.reshape(n_layers, n_phys, page * n_maps, d)
    cv = cache_v.reshape(n_layers, n_phys, page * n_heads, hw)

    def kmap(which):
        return lambda b, t, pt: (la, pt[b, 2 * jnp.minimum(t, n_steps - 1) + which], 0, 0)

    def vmap(which):
        return lambda b, t, pt: (la, pt[b, 2 * jnp.maximum(t - n_steps, 0) + which], 0, 0)

    kblock = (1, 1) + ck.shape[2:]
    vblock = (1, 1) + cv.shape[2:]
    per_seq = lambda shape: pl.BlockSpec((1,) + shape, lambda b, t, pt: (b, 0, 0))
    const = lambda shape: pl.BlockSpec(shape, lambda b, t, pt: (0, 0))
    grid_spec = pltpu.PrefetchScalarGridSpec(
        num_scalar_prefetch=1,
        grid=(bs, 2 * n_steps),
        in_specs=[pl.BlockSpec(memory_space=pltpu.SMEM),
                  per_seq((width, ncol)),
                  pl.BlockSpec(kblock, kmap(0)), pl.BlockSpec(kblock, kmap(1)),
                  pl.BlockSpec(vblock, vmap(0)), pl.BlockSpec(vblock, vmap(1)),
                  per_seq((n_new, width)), per_seq((n_new, width)), per_seq((n_new, width)),
                  const((page, ncol)), const((page, ncol)), const((1, ncol)), const((1, hw)),
                  const(perm.shape), const(expand.shape), const(mask.shape)],
        out_specs=per_seq((n_new, width)),
        scratch_shapes=[pltpu.VMEM((n_steps, 2 * page, ncol), F32), pltpu.VMEM((pad_new, ncol), F32),
                        pltpu.VMEM((1, ncol), F32), pltpu.VMEM((2, ncol // 2), F32),
                        pltpu.VMEM((n_heads * n_new, hw), F32), pltpu.VMEM((2 * page, width), BF16)],
    )
    return pl.pallas_call(
        functools.partial(_sample_attn_kernel, n_steps=n_steps, n_new=n_new,
                          scale=ATTN_HEAD_DIM ** -0.5, lam_init=lam_init),
        grid_spec=grid_spec,
        out_shape=jax.ShapeDtypeStruct((bs, n_new, width), F32),
        compiler_params=_cparams("parallel", "arbitrary"),
        name="sample_attention",
    )(page_table, lam, qbd, ck, ck, cv, cv, k_new, v_new, g,
      bias_last, bias_new, far, subln_w.reshape(1, hw), perm, expand, mask)


def _block_diag_queries(q, bs, n_new):
    d = ATTN_HEAD_DIM
    n_maps = q.shape[1] // d
    n_heads = n_maps // 2
    q5 = q.reshape(bs, n_new, n_heads, 2, d)
    qt = jnp.transpose(q5, (0, 2, 3, 4, 1))
    eye_h = jnp.eye(n_heads, dtype=q.dtype)
    eye_j = jnp.eye(2, dtype=q.dtype)
    out = (qt[:, :, :, :, None, None, :] * eye_j[None, None, :, None, :, None, None]
           * eye_h[None, :, None, None, None, :, None])
    return out.reshape(bs, n_maps * d, 2 * n_heads * n_new)


def _conv_kernel(x_ref, st_ref, w_ref, b_ref, y_ref, tail_ref, pad_ref, *, width):
    t = pl.program_id(2)
    tt = x_ref.shape[0]
    lo = 8 - (width - 1)

    @pl.when(t == 0)
    def _():
        pad_ref[lo:8, :] = st_ref[0]

    x = x_ref[...]
    pad_ref[8:8 + tt, :] = x
    acc = b_ref[...] + x * w_ref[width - 1:width, :]
    for j in range(width - 1):
        acc = acc + pad_ref[lo + j:lo + j + tt, :] * w_ref[j:j + 1, :]
    y_ref[...] = _silu(acc)
    tail = x[tt - (width - 1):, :]
    pad_ref[lo:8, :] = tail
    tail_ref[0] = tail


def _matmul_conv_kernel(a_ref, w_ref, cw_ref, cb_ref, y_ref, tail_ref, pad_ref, *, width, tiles_per_seq):
    i = pl.program_id(1)
    tm = a_ref.shape[0]
    lo = 8 - (width - 1)

    @pl.when(i % tiles_per_seq == 0)
    def _():
        pad_ref[lo:8, :] = jnp.zeros((width - 1, pad_ref.shape[1]), F32)

    chunk = min(2 * LANES, pad_ref.shape[1])
    for c0 in range(0, pad_ref.shape[1], chunk):
        cols = slice(c0, c0 + chunk)
        x = jnp.dot(a_ref[...], w_ref[:, cols], preferred_element_type=F32)
        pad_ref[8:8 + tm, cols] = x
        acc = cb_ref[:, cols] + x * cw_ref[width - 1:width, cols]
        for j in range(width - 1):
            acc = acc + pad_ref[lo + j:lo + j + tm, cols] * cw_ref[j:j + 1, cols]
        y_ref[:, cols] = _silu(acc)
        tail = x[tm - (width - 1):, :]
        pad_ref[lo:8, cols] = tail
        tail_ref[0, :, cols] = tail


def _matmul_conv(a, w, col0, conv_w, conv_b, batch, seq, tm, tn):
    m, k = a.shape
    width, c = conv_w.shape
    tm = min(tm, seq)
    assert seq % tm == 0 and c % tn == 0 and col0 % tn == 0 and tm >= width - 1
    cb0 = col0 // tn
    tiles_per_seq = seq // tm
    return pl.pallas_call(
        functools.partial(_matmul_conv_kernel, width=width, tiles_per_seq=tiles_per_seq),
        grid=(c // tn, m // tm),
        in_specs=[pl.BlockSpec((tm, k), lambda j, i: (i, 0)),
                  pl.BlockSpec((k, tn), lambda j, i: (0, cb0 + j)),
                  pl.BlockSpec((width, tn), lambda j, i: (0, j)),
                  pl.BlockSpec((1, tn), lambda j, i: (0, j))],
        out_specs=[pl.BlockSpec((tm, tn), lambda j, i: (i, j)),
                   pl.BlockSpec((1, width - 1, tn), lambda j, i: (i // tiles_per_seq, 0, j))],
        out_shape=[jax.ShapeDtypeStruct((m, c), F32),
                   jax.ShapeDtypeStruct((batch, width - 1, c), F32)],
        scratch_shapes=[pltpu.VMEM((8 + tm, tn), F32)],
        compiler_params=_cparams("parallel", "arbitrary"),
        name="matmul_conv",
    )(a, w, conv_w, conv_b.reshape(1, c))


def _conv(xbc, conv_state, conv_w, conv_b, batch, seq):
    m, c = xbc.shape
    width = conv_w.shape[0]
    tt = min(seq, 256)
    tc = 1024
    assert seq % tt == 0 and c % tc == 0 and tt >= width - 1
    nt = seq // tt
    return pl.pallas_call(
        functools.partial(_conv_kernel, width=width),
        grid=(batch, c // tc, nt),
        in_specs=[pl.BlockSpec((tt, tc), lambda b, j, t: (b * nt + t, j)),
                  pl.BlockSpec((1, width - 1, tc), lambda b, j, t: (b, 0, j)),
                  pl.BlockSpec((width, tc), lambda b, j, t: (0, j)),
                  pl.BlockSpec((1, tc), lambda b, j, t: (0, j))],
        out_specs=[pl.BlockSpec((tt, tc), lambda b, j, t: (b * nt + t, j)),
                   pl.BlockSpec((1, width - 1, tc), lambda b, j, t: (b, 0, j))],
        out_shape=[jax.ShapeDtypeStruct((m, c), F32),
                   jax.ShapeDtypeStruct((batch, width - 1, c), F32)],
        scratch_shapes=[pltpu.VMEM((8 + tt, tc), F32)],
        compiler_params=_cparams("parallel", "parallel", "arbitrary"),
        name="ssd_conv",
    )(xbc, conv_state, conv_w, conv_b.reshape(1, c))


def _split3(v):
    hi = v.astype(BF16)
    r1 = v - hi.astype(F32)
    mid = r1.astype(BF16)
    lo = (r1 - mid.astype(F32)).astype(BF16)
    return hi, mid, lo


def _dt_kernel(x_ref, bias_ref, alog_ref, dt_ref, cum_ref, dec_ref, *maybe_parts_ref):
    q = x_ref.shape[0]
    x = x_ref[...] + bias_ref[...]
    dt = jnp.maximum(x, 0.0) + jnp.log1p(jnp.exp(-jnp.abs(x)))
    c = dt * (-jnp.exp(alog_ref[...]))
    row = lax.broadcasted_iota(jnp.int32, c.shape, 0)
    shift = 1
    while shift < q:
        c = c + jnp.where(row >= shift, pltpu.roll(c, shift, axis=0), 0.0)
        shift *= 2
    dt_ref[...] = dt
    cum_ref[...] = c
    dec_ref[0] = jnp.exp(c[q - 1:q, :])
    if maybe_parts_ref:
        (parts_ref,) = maybe_parts_ref
        w_end = jnp.exp(c[q - 1:q, :] - c) * dt
        for i, v in enumerate((c, jnp.exp(c), w_end)):
            for j, part in enumerate(_split3(v)):
                parts_ref[3 * i + j] = part


def _dt_prep(dt_raw, dt_bias, a_log, q, with_parts):
    m, nh = dt_raw.shape
    row = pl.BlockSpec((q, nh), lambda i: (i, 0))
    vec = pl.BlockSpec((1, nh), lambda i: (0, 0))
    out_specs = [row, row, pl.BlockSpec((1, 1, nh), lambda i: (i, 0, 0))]
    out_shape = [jax.ShapeDtypeStruct((m, nh), F32), jax.ShapeDtypeStruct((m, nh), F32),
                 jax.ShapeDtypeStruct((m // q, 1, nh), F32)]
    if with_parts:
        out_specs.append(pl.BlockSpec((9, q, nh), lambda i: (0, i, 0)))
        out_shape.append(jax.ShapeDtypeStruct((9, m, nh), BF16))
    return pl.pallas_call(
        _dt_kernel,
        grid=(m // q,),
        in_specs=[row, vec, vec],
        out_specs=out_specs,
        out_shape=out_shape,
        compiler_params=_cparams("parallel"),
        name="ssd_dt_prep",
    )(dt_raw, dt_bias.reshape(1, nh), a_log.reshape(1, nh))


def _ssd_kernel(*refs, has_init, n_chunks):
    if has_init:
        (x_ref, b_ref, c_ref, z_ref, dt_ref, cum_ref, dtt_ref, cumt_ref, dec_ref, dskip_ref, gw_ref, s0_ref,
         y_ref, sout_ref, state_ref, ys_ref, xw_ref) = refs
    else:
        (x_ref, b_ref, c_ref, z_ref, dt_ref, cum_ref, dtt_ref, cumt_ref, dec_ref, dskip_ref, gw_ref,
         y_ref, sout_ref, state_ref, ys_ref, xw_ref) = refs
    ck = pl.program_id(2)
    q = x_ref.shape[0]
    r_heads = dt_ref.shape[3]
    p = SSM_HEAD_DIM

    @pl.when(ck == 0)
    def _():
        if has_init:
            state_ref[...] = s0_ref[0, 0].T
        else:
            state_ref[...] = jnp.zeros_like(state_ref)

    x = x_ref[...]
    xb = x.astype(BF16)
    bb = b_ref[...].astype(BF16)
    cb_ = c_ref[...].astype(BF16)
    cb = lax.dot_general(cb_, bb, (((1,), (1,)), ((), ())), preferred_element_type=F32)
    cs = jnp.dot(cb_, state_ref[...].astype(BF16), preferred_element_type=F32)
    dt = dt_ref[0, 0]
    cum = cum_ref[0, 0]
    dtt = dtt_ref[0, 0]
    cumt = cumt_ref[0, 0]
    w_end = jnp.exp(cum[q - 1:q, :] - cum) * dt
    ecum = jnp.exp(cum)
    li = lax.broadcasted_iota(jnp.int32, (q, q), 0)
    si = lax.broadcasted_iota(jnp.int32, (q, q), 1)
    causal = li >= si
    for r in range(r_heads):
        cols = slice(r * p, (r + 1) * p)
        seg = cum[:, r:r + 1] - cumt[r:r + 1, :]
        decay = jnp.exp(jnp.where(causal, seg, NEG_INF))
        mh = (cb * decay * dtt[r:r + 1, :]).astype(BF16)
        ys_ref[:, cols] = (jnp.dot(mh, xb[:, cols], preferred_element_type=F32)
                           + cs[:, cols] * ecum[:, r:r + 1])
        xw_ref[:, cols] = (x[:, cols] * w_end[:, r:r + 1]).astype(BF16)
    ds = lax.dot_general(bb, xw_ref[...], (((0,), (0,)), ((), ())), preferred_element_type=F32)
    state_ref[...] = state_ref[...] * dec_ref[0, 0, 0] + ds

    y = ys_ref[...] + x * dskip_ref[...]
    y = y * _silu(z_ref[...])
    y = y * lax.rsqrt(jnp.mean(y * y, axis=-1, keepdims=True) + NORM_EPS) * gw_ref[...]
    y_ref[...] = y.astype(y_ref.dtype)

    @pl.when(ck == n_chunks - 1)
    def _():
        sout_ref[0, 0] = state_ref[...].T


def _ssd_scan(act, z, dt, cum, dec_last, d_skip, gnorm_w, state0, batch, seq, q, out_dtype):
    m = act.shape[0]
    inner = z.shape[1]
    g = SSM_GROUPS
    n = SSM_STATE
    gw_cols = inner // g
    r = gw_cols // SSM_HEAD_DIM
    nc = seq // q
    assert n == LANES and gw_cols % LANES == 0
    dt4 = jnp.transpose(dt.reshape(batch, seq, g, r), (0, 2, 1, 3))
    cum4 = jnp.transpose(cum.reshape(batch, seq, g, r), (0, 2, 1, 3))
    dtt4 = jnp.transpose(dt4, (0, 1, 3, 2))
    cumt4 = jnp.transpose(cum4, (0, 1, 3, 2))
    dec5 = jnp.repeat(dec_last.reshape(batch, nc, g, 1, r), SSM_HEAD_DIM, axis=-1)
    dskip_row = jnp.repeat(d_skip, SSM_HEAD_DIM).reshape(1, inner)

    xcol0 = inner // gw_cols
    row_blk = lambda off: pl.BlockSpec((q, gw_cols), lambda b, gi, c: (b * nc + c, off + gi))
    bspec = pl.BlockSpec((q, n), lambda b, gi, c: (b * nc + c, inner // n + gi))
    cspec = pl.BlockSpec((q, n), lambda b, gi, c: (b * nc + c, inner // n + g + gi))
    tcol = pl.BlockSpec((1, 1, q, r), lambda b, gi, c: (b, gi, c, 0))
    trow = pl.BlockSpec((1, 1, r, q), lambda b, gi, c: (b, gi, 0, c))
    vec = pl.BlockSpec((1, gw_cols), lambda b, gi, c: (0, gi))
    sspec = pl.BlockSpec((1, 1, gw_cols, n), lambda b, gi, c: (b, gi, 0, 0))
    in_specs = [row_blk(0), bspec, cspec, row_blk(0), tcol, tcol, trow, trow,
                pl.BlockSpec((1, 1, 1, 1, gw_cols), lambda b, gi, c: (b, c, gi, 0, 0)), vec, vec]
    args = [act, act, act, z, dt4, cum4, dtt4, cumt4, dec5, dskip_row, gnorm_w.reshape(1, inner)]
    if state0 is not None:
        in_specs.append(sspec)
        args.append(state0.reshape(batch, g, gw_cols, n))
    del xcol0
    y, s_out = pl.pallas_call(
        functools.partial(_ssd_kernel, has_init=state0 is not None, n_chunks=nc),
        grid=(batch, g, nc),
        in_specs=in_specs,
        out_specs=[row_blk(0), sspec],
        out_shape=[jax.ShapeDtypeStruct((m, inner), out_dtype),
                   jax.ShapeDtypeStruct((batch, g, gw_cols, n), F32)],
        scratch_shapes=[pltpu.VMEM((n, gw_cols), F32), pltpu.VMEM((q, gw_cols), F32),
                        pltpu.VMEM((q, gw_cols), BF16)],
        compiler_params=_cparams("parallel", "parallel", "arbitrary"),
        name="ssd_scan_stateful",
    )(*args)
    return y, s_out


def _expansion_matrix(r_heads):
    p = SSM_HEAD_DIM
    e = np.zeros((9 * r_heads, r_heads * (LANES + 2 * p)), np.float32)
    for k in range(9):
        for r in range(r_heads):
            if k < 3:
                lo, w = r * LANES, LANES
            else:
                lo, w = r_heads * LANES + (k // 3 - 1) * r_heads * p + r * p, p
            e[k * r_heads + r, lo:lo + w] = 1.0
    return jnp.asarray(e, BF16)


def _ssd_chunk_kernel(x_ref, b_ref, c_ref, z_ref, parts_ref, dtt_ref, cumt_ref, e_ref, dec_ref, dskip_ref, gw_ref,
                      y_ref, sout_ref, state_ref, xl_ref, ys_ref, *, n_chunks):
    ck = pl.program_id(2)
    q = x_ref.shape[0]
    r_heads = dtt_ref.shape[2]
    p = SSM_HEAD_DIM

    @pl.when(ck == 0)
    def _():
        state_ref[...] = jnp.zeros_like(state_ref)

    x = x_ref[...]
    bb = b_ref[...].astype(BF16)
    cb_ = c_ref[...].astype(BF16)
    cb = lax.dot_general(cb_, bb, (((1,), (1,)), ((), ())), preferred_element_type=F32)
    cs = jnp.dot(cb_, state_ref[...].astype(BF16), preferred_element_type=F32)
    xl_ref[...] = jnp.dot(parts_ref[0, 0], e_ref[...], preferred_element_type=F32)
    dtt = dtt_ref[0, 0]
    cumt = cumt_ref[0, 0]
    li = lax.broadcasted_iota(jnp.int32, (q, q), 0)
    si = lax.broadcasted_iota(jnp.int32, (q, q), 1)
    causal = li >= si
    first = lax.broadcasted_iota(jnp.int32, (q, 2 * p), 1) < p
    for pair in range(r_heads // 2):
        mhs = []
        for r in (2 * pair, 2 * pair + 1):
            seg = xl_ref[:, r * LANES:(r + 1) * LANES] - cumt[r:r + 1, :]
            decay = jnp.exp(jnp.where(causal, seg, NEG_INF))
            mhs.append((cb * decay * dtt[r:r + 1, :]).astype(BF16))
        xp = x[:, pair * 2 * p:(pair + 1) * 2 * p]
        rhs = jnp.concatenate([jnp.where(first, xp, 0.0).astype(BF16),
                               jnp.where(first, 0.0, xp).astype(BF16)], axis=0)
        ys_ref[:, pair * 2 * p:(pair + 1) * 2 * p] = jnp.dot(jnp.concatenate(mhs, axis=1), rhs,
                                                             preferred_element_type=F32)
    c0 = r_heads * LANES
    ecum = xl_ref[:, c0:c0 + r_heads * p]
    w_end = xl_ref[:, c0 + r_heads * p:c0 + 2 * r_heads * p]
    xw = (x * w_end).astype(BF16)
    ds = lax.dot_general(bb, xw, (((0,), (0,)), ((), ())), preferred_element_type=F32)
    state_ref[...] = state_ref[...] * dec_ref[0, 0, 0] + ds

    y = ys_ref[...] + cs * ecum + x * dskip_ref[...]
    y = y * _silu(z_ref[...])
    y = y * lax.rsqrt(jnp.mean(y * y, axis=-1, keepdims=True) + NORM_EPS) * gw_ref[...]
    y_ref[...] = y.astype(y_ref.dtype)

    @pl.when(ck == n_chunks - 1)
    def _():
        sout_ref[0, 0] = state_ref[...].T


def _ssd_scan_chunked(act, z, dt, cum, dec_last, parts, d_skip, gnorm_w, batch, seq, q, out_dtype):
    m = act.shape[0]
    inner = z.shape[1]
    g = SSM_GROUPS
    n = SSM_STATE
    gw_cols = inner // g
    r = gw_cols // SSM_HEAD_DIM
    nc = seq // q
    assert n == LANES and q == LANES and gw_cols % LANES == 0 and r % 2 == 0
    dtt4 = jnp.transpose(dt.reshape(batch, seq, g, r), (0, 2, 3, 1))
    cumt4 = jnp.transpose(cum.reshape(batch, seq, g, r), (0, 2, 3, 1))
    parts4 = jnp.transpose(parts.reshape(9, batch, seq, g, r), (1, 3, 2, 0, 4)).reshape(batch, g, seq, 9 * r)
    dec5 = jnp.repeat(dec_last.reshape(batch, nc, g, 1, r), SSM_HEAD_DIM, axis=-1)
    dskip_row = jnp.repeat(d_skip, SSM_HEAD_DIM).reshape(1, inner)
    emat = _expansion_matrix(r)

    row_blk = pl.BlockSpec((q, gw_cols), lambda b, gi, c: (b * nc + c, gi))
    bspec = pl.BlockSpec((q, n), lambda b, gi, c: (b * nc + c, inner // n + gi))
    cspec = pl.BlockSpec((q, n), lambda b, gi, c: (b * nc + c, inner // n + g + gi))
    trow = pl.BlockSpec((1, 1, r, q), lambda b, gi, c: (b, gi, 0, c))
    vec = pl.BlockSpec((1, gw_cols), lambda b, gi, c: (0, gi))
    sspec = pl.BlockSpec((1, 1, gw_cols, n), lambda b, gi, c: (b, gi, 0, 0))
    y, s_out = pl.pallas_call(
        functools.partial(_ssd_chunk_kernel, n_chunks=nc),
        grid=(batch, g, nc),
        in_specs=[row_blk, bspec, cspec, row_blk,
                  pl.BlockSpec((1, 1, q, 9 * r), lambda b, gi, c: (b, gi, c, 0)), trow, trow,
                  pl.BlockSpec(emat.shape, lambda b, gi, c: (0, 0)),
                  pl.BlockSpec((1, 1, 1, 1, gw_cols), lambda b, gi, c: (b, c, gi, 0, 0)), vec, vec],
        out_specs=[row_blk, sspec],
        out_shape=[jax.ShapeDtypeStruct((m, inner), out_dtype),
                   jax.ShapeDtypeStruct((batch, g, gw_cols, n), F32)],
        scratch_shapes=[pltpu.VMEM((n, gw_cols), F32), pltpu.VMEM((q, emat.shape[1]), F32),
                        pltpu.VMEM((q, gw_cols), F32)],
        compiler_params=_cparams("parallel", "parallel", "arbitrary"),
        name="ssd_scan_chunked",
    )(act, act, act, z, parts4, dtt4, cumt4, emat, dec5, dskip_row, gnorm_w.reshape(1, inner))
    return y, s_out


def _attn_layer(hp, hs, w_in, w_out, lambda_qk_l, subln_w_l, rel_bias, cache_k, cache_v, la, page_table,
                batch, seq, bs, n_new, lam_init):
    width = w_out.shape[0]
    tile = min(seq, 512)
    assert seq % tile == 0 and tile + 1 >= FAR_DISTANCE
    page = cache_k.shape[2]

    bias_tiles = _bias_tiles(rel_bias, tile)
    bias_last, bias_new, far, lam = _sample_prep(rel_bias, lambda_qk_l, page, n_new, lam_init)

    (q,) = _matmul(hp, w_in, 0, width, [BF16], 512, 1024)
    k, kb = _matmul(hp, w_in, width, width, [F32, BF16], 512, 1024)
    v, vb = _matmul(hp, w_in, 2 * width, width, [F32, BF16], 512, 1024)
    (g,) = _matmul(hp, w_in, 3 * width, width, [F32], 512, 1024)
    og = _prompt_attention(lam, rel_bias, q, kb, vb, g, bias_tiles, subln_w_l, batch, seq, tile, lam_init)
    (op,) = _matmul(og, w_out, 0, w_out.shape[1], [F32], 512, 1024)

    (q2,) = _matmul(hs, w_in, 0, width, [BF16], 512, 1024)
    (k2,) = _matmul(hs, w_in, width, width, [F32], 512, 1024)
    (v2,) = _matmul(hs, w_in, 2 * width, width, [F32], 512, 1024)
    (g2,) = _matmul(hs, w_in, 3 * width, width, [F32], 512, 1024)
    qbd = _block_diag_queries(q2, bs, n_new)
    og2 = _sample_attention(page_table, lam, qbd, cache_k, cache_v, la,
                            k2.reshape(bs, n_new, width), v2.reshape(bs, n_new, width),
                            g2.reshape(bs, n_new, width), bias_last, bias_new, far, subln_w_l, n_new, lam_init)
    (os_,) = _matmul(og2.reshape(bs * n_new, width).astype(BF16), w_out, 0, w_out.shape[1], [F32], 512, 1024)
    return op, os_, k, v, k2, v2


def _ssd_layer(h, conv_state, ssm_state, w_in, conv_w, conv_b, dt_bias, a_log, d_skip, gnorm_w, w_out,
               batch, seq):
    inner = w_out.shape[0]
    conv_dim = conv_w.shape[1]
    n_heads = dt_bias.shape[0]
    q = SSD_CHUNK if seq % SSD_CHUNK == 0 else seq
    (z,) = _matmul(h, w_in, 0, inner, [F32], 512, 1024)
    (dt_raw,) = _matmul(h, w_in, inner + conv_dim, n_heads, [F32], 512, n_heads)
    if conv_state is None:
        act, conv_out = _matmul_conv(h, w_in, inner, conv_w, conv_b, batch, seq, 512, 1024)
    else:
        (xbc,) = _matmul(h, w_in, inner, conv_dim, [F32], 512, 1024)
        act, conv_out = _conv(xbc, conv_state, conv_w, conv_b, batch, seq)
    if ssm_state is None and q == LANES:
        dt, cum, dec_last, parts = _dt_prep(dt_raw, dt_bias, a_log, q, True)
        y, s_out = _ssd_scan_chunked(act, z, dt, cum, dec_last, parts, d_skip, gnorm_w, batch, seq, q, BF16)
    else:
        dt, cum, dec_last = _dt_prep(dt_raw, dt_bias, a_log, q, False)
        y, s_out = _ssd_scan(act, z, dt, cum, dec_last, d_skip, gnorm_w, ssm_state, batch, seq, q, F32)
    (o,) = _matmul(y.astype(BF16), w_out, 0, w_out.shape[1], [F32], 512, 512)
    return o, conv_out, s_out.reshape(batch, n_heads, SSM_HEAD_DIM, SSM_STATE)


def kernel(x_prompt, x_sample, cache_k, cache_v, page_table, state_conv, state_ssm, norm_pre, norm_post, rel_bias,
           w_attn_in, lambda_qk, subln_w, w_attn_out, w_ssm_in, conv_w, conv_b, dt_bias, a_log, d_skip, gnorm_w,
           w_ssm_out):
    bp, sp, dm = x_prompt.shape
    bs, ss, _ = x_sample.shape
    depth = norm_pre.shape[0]
    xp = x_prompt.reshape(bp * sp, dm)
    xs = x_sample.reshape(bs * ss, dm)
    kp_l, vp_l, ks_l, vs_l, cp_l, sp_l, cs_l, ss_l = [], [], [], [], [], [], [], []
    for i in range(depth):
        hp = _rmsnorm(xp, norm_pre[i], NORM_EPS, BF16)
        hs = _rmsnorm(xs, norm_pre[i], NORM_EPS, BF16)
        if i % N_MIXERS == 0:
            la = i // N_MIXERS
            lam_init = 0.8 - 0.6 * math.exp(-0.3 * i)
            op, os_, k, v, k2, v2 = _attn_layer(
                hp, hs, w_attn_in[la].astype(BF16), w_attn_out[la].astype(BF16), lambda_qk[la], subln_w[la],
                rel_bias, cache_k, cache_v, la, page_table, bp, sp, bs, ss, lam_init)
            n_maps = k.shape[1] // ATTN_HEAD_DIM
            kp_l.append(k.reshape(bp, sp, n_maps, ATTN_HEAD_DIM))
            vp_l.append(v.reshape(bp, sp, n_maps // 2, 2 * ATTN_HEAD_DIM))
            ks_l.append(k2.reshape(bs, ss, n_maps, ATTN_HEAD_DIM))
            vs_l.append(v2.reshape(bs, ss, n_maps // 2, 2 * ATTN_HEAD_DIM))
        else:
            ls = i // N_MIXERS
            wts = (w_ssm_in[ls].astype(BF16), conv_w[ls], conv_b[ls], dt_bias[ls], a_log[ls], d_skip[ls],
                   gnorm_w[ls], w_ssm_out[ls].astype(BF16))
            op, cpn, spn = _ssd_layer(hp, None, None, *wts, bp, sp)
            os_, csn, ssn = _ssd_layer(hs, state_conv[ls], state_ssm[ls], *wts, bs, ss)
            cp_l.append(cpn)
            sp_l.append(spn)
            cs_l.append(csn)
            ss_l.append(ssn)
        xp = _residual_norm(xp, op, norm_post[i], NORM_EPS)
        xs = _residual_norm(xs, os_, norm_post[i], NORM_EPS)
    return (xp.reshape(bp, sp, dm), xs.reshape(bs, ss, dm),
            jnp.stack(kp_l), jnp.stack(vp_l), jnp.stack(ks_l), jnp.stack(vs_l),
            jnp.stack(cp_l), jnp.stack(sp_l), jnp.stack(cs_l), jnp.stack(ss_l))
```

```python
import functools
import math

import numpy as np
import jax
import jax.numpy as jnp
from jax import lax
from jax.experimental import pallas as pl
from jax.experimental.pallas import tpu as pltpu

ATTN_HEAD_DIM = 128
N_BUCKETS = 32
MAX_DISTANCE = 128
SUBLN_EPS = 1e-5
NORM_EPS = 1e-6
SSM_HEAD_DIM = 64
SSM_GROUPS = 8
SSM_STATE = 128
CONV_WIDTH = 4
SSD_CHUNK = 128
N_MIXERS = 2

V7X_VMEM_LIMIT_BYTES = 56 * 1024 * 1024
LANES = 128

F32 = jnp.float32
BF16 = jnp.bfloat16
NEG_INF = float("-inf")


def _cparams(*sem):
    return pltpu.CompilerParams(dimension_semantics=sem, vmem_limit_bytes=V7X_VMEM_LIMIT_BYTES)


def _bucket_thresholds():
    n = np.arange(0, 4 * MAX_DISTANCE)
    max_exact = N_BUCKETS // 2
    nf = np.maximum(n, max_exact).astype(np.float32)
    large = max_exact + (np.log(nf / max_exact) / math.log(MAX_DISTANCE / max_exact)
                         * (N_BUCKETS - max_exact)).astype(np.int32)
    bucket = np.where(n < max_exact, n, np.minimum(large, N_BUCKETS - 1))
    return [int(np.argmax(bucket >= b)) for b in range(N_BUCKETS)]


BUCKET_THR = _bucket_thresholds()
FAR_DISTANCE = BUCKET_THR[N_BUCKETS - 1]


def _bias_of_distance(n, rel_of_bucket):
    v = jnp.broadcast_to(rel_of_bucket(0), n.shape).astype(F32)
    for b in range(1, N_BUCKETS):
        v = jnp.where(n >= BUCKET_THR[b], rel_of_bucket(b), v)
    return v


def _bias_tiles_kernel(rel_ref, out_ref, *, tile):
    h = pl.program_id(0)
    i = lax.broadcasted_iota(jnp.int32, (tile, tile), 0)
    j = lax.broadcasted_iota(jnp.int32, (tile, tile), 1)
    d = i - j
    rel = lambda b: rel_ref[b, h]
    out_ref[0, 0] = jnp.where(d >= 0, _bias_of_distance(d, rel), NEG_INF)
    out_ref[0, 1] = _bias_of_distance(d + tile, rel)


def _bias_tiles(rel_bias, tile):
    n_heads = rel_bias.shape[1]
    return pl.pallas_call(
        functools.partial(_bias_tiles_kernel, tile=tile),
        grid=(n_heads,),
        in_specs=[pl.BlockSpec(memory_space=pltpu.SMEM)],
        out_specs=pl.BlockSpec((1, 2, tile, tile), lambda h: (h, 0, 0, 0)),
        out_shape=jax.ShapeDtypeStruct((n_heads, 2, tile, tile), F32),
        compiler_params=_cparams("arbitrary"),
        name="bias_tiles",
    )(rel_bias)


def _sample_prep_kernel(relc_ref, lq_ref, bias_last_ref, bias_new_ref, far_ref, lam_ref, *, page, n_new, lam_init):
    ncol = relc_ref.shape[1]
    key = lax.broadcasted_iota(jnp.int32, (page, ncol), 0)
    col = lax.broadcasted_iota(jnp.int32, (page, ncol), 1)
    qi = col % n_new
    rel = lambda b: relc_ref[b:b + 1, :]
    bias_last_ref[...] = _bias_of_distance(page + qi - key, rel)
    d = qi - key
    bias_new_ref[...] = jnp.where(d >= 0, _bias_of_distance(d, rel), NEG_INF)
    far_ref[...] = relc_ref[N_BUCKETS - 1:N_BUCKETS, :]
    lq = lq_ref[...]
    lam = (jnp.exp(jnp.sum(lq[0:1] * lq[1:2], axis=-1, keepdims=True))
           - jnp.exp(jnp.sum(lq[2:3] * lq[3:4], axis=-1, keepdims=True)) + lam_init)
    lam_ref[...] = lam


def _sample_prep(rel_bias, lambda_qk_l, page, n_new, lam_init):
    n_heads = rel_bias.shape[1]
    ncol = 2 * n_heads * n_new
    relc = jnp.tile(jnp.repeat(rel_bias, n_new, axis=1), (1, 2))
    full = lambda shape: pl.BlockSpec(shape, lambda: (0,) * len(shape))
    return pl.pallas_call(
        functools.partial(_sample_prep_kernel, page=page, n_new=n_new, lam_init=lam_init),
        in_specs=[full(relc.shape), full(lambda_qk_l.shape)],
        out_specs=[full((page, ncol)), full((page, ncol)), full((1, ncol)), full((1, 1))],
        out_shape=[jax.ShapeDtypeStruct((page, ncol), F32), jax.ShapeDtypeStruct((page, ncol), F32),
                   jax.ShapeDtypeStruct((1, ncol), F32), jax.ShapeDtypeStruct((1, 1), F32)],
    )(relc, lambda_qk_l)


def _rmsnorm_kernel(x_ref, w_ref, o_ref, *, eps):
    x = x_ref[...]
    y = x * lax.rsqrt(jnp.mean(x * x, axis=-1, keepdims=True) + eps)
    o_ref[...] = (y * w_ref[...]).astype(o_ref.dtype)


def _rmsnorm(x2d, w, eps, out_dtype):
    m, d = x2d.shape
    tm = min(m, 256)
    return pl.pallas_call(
        functools.partial(_rmsnorm_kernel, eps=eps),
        grid=(m // tm,),
        in_specs=[pl.BlockSpec((tm, d), lambda i: (i, 0)), pl.BlockSpec((1, d), lambda i: (0, 0))],
        out_specs=pl.BlockSpec((tm, d), lambda i: (i, 0)),
        out_shape=jax.ShapeDtypeStruct((m, d), out_dtype),
        compiler_params=_cparams("parallel"),
        name="rmsnorm",
    )(x2d, w.reshape(1, d))


def _residual_norm_kernel(x_ref, y_ref, w_ref, o_ref, *, eps):
    y = y_ref[...]
    yn = y * lax.rsqrt(jnp.mean(y * y, axis=-1, keepdims=True) + eps)
    o_ref[...] = x_ref[...] + yn * w_ref[...]


def _residual_norm(x2d, y2d, w, eps):
    m, d = x2d.shape
    tm = min(m, 256)
    row = pl.BlockSpec((tm, d), lambda i: (i, 0))
    return pl.pallas_call(
        functools.partial(_residual_norm_kernel, eps=eps),
        grid=(m // tm,),
        in_specs=[row, row, pl.BlockSpec((1, d), lambda i: (0, 0))],
        out_specs=row,
        out_shape=jax.ShapeDtypeStruct((m, d), F32),
        compiler_params=_cparams("parallel"),
        name="residual_norm",
    )(x2d, y2d, w.reshape(1, d))


def _matmul_kernel(a_ref, *refs, w_parts):
    w_refs, o_refs = refs[:w_parts], refs[w_parts:]
    tw = w_refs[0].shape[1]
    for p, w_ref in enumerate(w_refs):
        acc = jnp.dot(a_ref[...], w_ref[...], preferred_element_type=F32)
        for o_ref in o_refs:
            o_ref[:, p * tw:(p + 1) * tw] = acc.astype(o_ref.dtype)


SKINNY_MATMUL_WEIGHT_PARTS = 4
SKINNY_MATMUL_MAX_ROWS = 128


def _matmul(a, w, col0, n, out_dtypes, tm, tn):
    m, k = a.shape
    tm = min(tm, m)
    tn = min(tn, n)
    assert m % tm == 0 and n % tn == 0 and col0 % tn == 0
    skinny = m <= SKINNY_MATMUL_MAX_ROWS and tn % (SKINNY_MATMUL_WEIGHT_PARTS * LANES) == 0
    w_parts = SKINNY_MATMUL_WEIGHT_PARTS if skinny else 1
    tw = tn // w_parts
    cb0 = col0 // tw
    w_spec = lambda p: pl.BlockSpec((k, tw), lambda j, i: (0, cb0 + j * w_parts + p))
    outs = pl.pallas_call(
        functools.partial(_matmul_kernel, w_parts=w_parts),
        grid=(n // tn, m // tm),
        in_specs=[pl.BlockSpec((tm, k), lambda j, i: (i, 0))] + [w_spec(p) for p in range(w_parts)],
        out_specs=[pl.BlockSpec((tm, tn), lambda j, i: (i, j)) for _ in out_dtypes],
        out_shape=[jax.ShapeDtypeStruct((m, n), dt) for dt in out_dtypes],
        compiler_params=_cparams("parallel", "parallel"),
        name="matmul",
    )(a, *([w] * w_parts))
    return outs


def _silu(g):
    return g * (1.0 / (1.0 + jnp.exp(-g)))


def _subln_gate(o, g, sw, lam_init):
    o = o * lax.rsqrt(jnp.mean(o * o, axis=-1, keepdims=True) + SUBLN_EPS) * sw
    return o * (1.0 - lam_init) * _silu(g)


def _prompt_attn_kernel(lam_ref, rel_ref, q_ref, k_ref, v_ref, g_ref, bias_ref, sw_ref, o_ref,
                        m_ref, l_ref, acc_ref, *, tile, scale, lam_init):
    h = pl.program_id(1)
    qb = pl.program_id(2)
    d = ATTN_HEAD_DIM
    far = rel_ref[N_BUCKETS - 1, h]

    def scores(j, kblk, bias):
        s = lax.dot_general(q_ref[:, j * d:(j + 1) * d], kblk[:, j * d:(j + 1) * d],
                            (((1,), (1,)), ((), ())), preferred_element_type=F32)
        return s * scale + bias

    def row_reduce(x, fold, reduce):
        acc = x[:, 0:LANES]
        for i in range(1, x.shape[1] // LANES):
            acc = fold(acc, x[:, i * LANES:(i + 1) * LANES])
        return reduce(acc, axis=-1, keepdims=True)

    def block(row0, bias, first):
        kblk = k_ref[pl.ds(row0, tile), :]
        vblk = v_ref[pl.ds(row0, tile), :]
        for j in range(2):
            s = scores(j, kblk, bias)
            row_max = row_reduce(s, jnp.maximum, jnp.max)
            m = row_max if first else jnp.maximum(m_ref[j], row_max)
            p = jnp.exp(s - m)
            row_sum = row_reduce(p, jnp.add, jnp.sum)
            pv = jnp.dot(p.astype(BF16), vblk, preferred_element_type=F32)
            if first:
                l_ref[j] = row_sum
                acc_ref[j] = pv
            else:
                alpha = jnp.exp(m_ref[j] - m)
                l_ref[j] = alpha * l_ref[j] + row_sum
                acc_ref[j] = alpha * acc_ref[j] + pv
            m_ref[j] = m

    first_block = functools.partial(block, first=True)
    next_block = functools.partial(block, first=False)

    first_block(pl.multiple_of(qb * tile, tile), bias_ref[0, 0])

    @pl.when(qb >= 1)
    def _():
        next_block(pl.multiple_of((qb - 1) * tile, tile), bias_ref[0, 1])

    def far_body(kb, carry):
        next_block(pl.multiple_of(kb * tile, tile), far)
        return carry

    lax.fori_loop(0, jnp.maximum(qb - 1, 0), far_body, 0)

    lam = lam_ref[0, 0]
    o = acc_ref[0] / l_ref[0] - lam * (acc_ref[1] / l_ref[1])
    o_ref[...] = _subln_gate(o, g_ref[...], sw_ref[...], lam_init).astype(o_ref.dtype)


def _prompt_attention(lam, rel_bias, q, k, v, g, bias_tiles, subln_w, batch, seq, tile, lam_init):
    m, width = q.shape
    hw = 2 * ATTN_HEAD_DIM
    n_heads = width // hw
    nq = seq // tile
    smem = pl.BlockSpec(memory_space=pltpu.SMEM)
    qspec = pl.BlockSpec((tile, hw), lambda b, h, i: (b * nq + i, h))
    kvspec = pl.BlockSpec((seq, hw), lambda b, h, i: (b, h))
    return pl.pallas_call(
        functools.partial(_prompt_attn_kernel, tile=tile, scale=ATTN_HEAD_DIM ** -0.5, lam_init=lam_init),
        grid=(batch, n_heads, nq),
        in_specs=[smem, smem, qspec, kvspec, kvspec, qspec,
                  pl.BlockSpec((1, 2, tile, tile), lambda b, h, i: (h, 0, 0, 0)),
                  pl.BlockSpec((1, hw), lambda b, h, i: (0, 0))],
        out_specs=qspec,
        out_shape=jax.ShapeDtypeStruct((m, width), BF16),
        scratch_shapes=[pltpu.VMEM((2, tile, 1), F32), pltpu.VMEM((2, tile, 1), F32),
                        pltpu.VMEM((2, tile, hw), F32)],
        compiler_params=_cparams("parallel", "parallel", "arbitrary"),
        name="prompt_attention",
    )(lam, rel_bias, q, k, v, g, bias_tiles, subln_w.reshape(1, hw))


SUBLANES = 8
SAMPLE_PAGE_PARTS = 4


def _sample_attn_constants(page, n_maps, n_heads, n_new):
    rb = SUBLANES * n_maps
    perm = np.zeros((rb, rb), np.float32)
    for c in range(n_maps):
        for k8 in range(SUBLANES):
            perm[c * SUBLANES + k8, k8 * n_maps + c] = 1.0
    expand = np.zeros((page, page * n_heads), np.float32)
    for key in range(page):
        expand[key, key * n_heads:(key + 1) * n_heads] = 1.0
    row_head = np.arange(n_heads * n_new)[:, None] // n_new
    col_head = np.arange(page * n_heads)[None, :] % n_heads
    mask = (row_head == col_head).astype(np.float32)
    return jnp.asarray(perm, BF16), jnp.asarray(expand, BF16), jnp.asarray(mask, F32)


def _sample_attn_kernel(pt_ref, lam_ref, qbd_ref, *refs, n_parts, n_steps, n_new, scale, lam_init):
    ka_refs, kb_refs = refs[0:n_parts], refs[n_parts:2 * n_parts]
    va_refs, vb_refs = refs[2 * n_parts:3 * n_parts], refs[3 * n_parts:4 * n_parts]
    (kn_ref, vn_ref, g_ref, bias_last_ref, bias_new_ref, far_ref, sw_ref, perm_ref, expand_ref, mask_ref, o_ref,
     s_ref, sn_ref, m_ref, w_ref, acc_ref, kscr_ref) = refs[4 * n_parts:]
    t = pl.program_id(1)
    d = ATTN_HEAD_DIM
    rb = perm_ref.shape[0]
    n_maps = rb // SUBLANES
    page = n_parts * ka_refs[0].shape[2] // n_maps

    def page_rows(part_refs):
        return jnp.concatenate([r[0, 0] for r in part_refs], axis=0)

    n_blocks = page // SUBLANES
    half = s_ref.shape[2] // 2
    hw = 2 * d
    n_heads = n_maps // 2
    pad_new = sn_ref.shape[0]

    def gather_keys(k_refs, row0):
        kb16 = page_rows(k_refs).astype(BF16)
        w = jnp.concatenate([kb16[b * rb:(b + 1) * rb] for b in range(n_blocks)], axis=1)
        out = jnp.dot(perm_ref[...], w, preferred_element_type=F32)

        def block_rows(b):
            return jnp.concatenate([out[c * SUBLANES:(c + 1) * SUBLANES, b * d:(b + 1) * d] for c in range(n_maps)],
                                   axis=1)

        for b in range(0, n_blocks, 2):
            r0 = row0 + b * SUBLANES
            kscr_ref[r0:r0 + 2 * SUBLANES, :] = jnp.concatenate([block_rows(b), block_rows(b + 1)],
                                                                 axis=0).astype(BF16)

    @pl.when(t < n_steps)
    def _():
        gather_keys(ka_refs, 0)
        gather_keys(kb_refs, page)
        s = jnp.dot(kscr_ref[...], qbd_ref[0], preferred_element_type=F32) * scale
        s_ref[t, 0:page, :] = s[0:page] + far_ref[...]
        s_ref[t, page:2 * page, :] = s[page:] + jnp.where(t == n_steps - 1, bias_last_ref[...], far_ref[...])
        pmax = jnp.max(s_ref[t], axis=0, keepdims=True)

        @pl.when(t == 0)
        def _():
            m_ref[...] = pmax

        @pl.when(t > 0)
        def _():
            m_ref[...] = jnp.maximum(m_ref[...], pmax)

    @pl.when(t == n_steps - 1)
    def _():
        kn = jnp.concatenate([kn_ref[0], jnp.zeros((pad_new - n_new, kn_ref.shape[2]), F32)], axis=0)
        sn = (jnp.dot(kn.astype(BF16), qbd_ref[0], preferred_element_type=F32) * scale
              + bias_new_ref[0:pad_new, :])
        m = jnp.maximum(m_ref[...], jnp.max(sn, axis=0, keepdims=True))
        pn = jnp.exp(sn - m)
        sn_ref[...] = pn

        def body(slot, l):
            p = jnp.exp(s_ref[slot] - m)
            s_ref[slot] = p
            return l + jnp.sum(p, axis=0, keepdims=True)

        l = lax.fori_loop(0, n_steps, body, jnp.sum(pn, axis=0, keepdims=True))
        inv = 1.0 / l
        w_ref[0:1, :] = inv[:, :half]
        w_ref[1:2, :] = lam_ref[0, 0] * inv[:, half:]
        acc_ref[...] = jnp.zeros_like(acc_ref)

    def combined_probs_t(p):
        a = p[:, :half] * w_ref[0:1, :] - p[:, half:] * w_ref[1:2, :]
        return a.T.astype(BF16)

    def accumulate_page(p, v_refs):
        at = combined_probs_t(p)
        spread = jnp.dot(at, expand_ref[...], preferred_element_type=F32) * mask_ref[...]
        acc_ref[...] += jnp.dot(spread.astype(BF16), page_rows(v_refs).astype(BF16), preferred_element_type=F32)

    @pl.when(t >= n_steps)
    def _():
        p = s_ref[t - n_steps]
        accumulate_page(p[0:page], va_refs)
        accumulate_page(p[page:], vb_refs)

    @pl.when(t == 2 * n_steps - 1)
    def _():
        pn = jnp.concatenate([sn_ref[...], jnp.zeros((page - pad_new, 2 * half), F32)], axis=0)
        at = combined_probs_t(pn)
        rows = 2 * n_new
        for h in range(n_heads):
            vh = jnp.concatenate([vn_ref[0, :, h * hw:(h + 1) * hw], jnp.zeros((page - n_new, hw), F32)], axis=0)
            r0 = (h * n_new) // rows * rows
            res = jnp.dot(at[r0:r0 + rows, :], vh.astype(BF16), preferred_element_type=F32)
            off = h * n_new - r0
            oh = acc_ref[h * n_new:(h + 1) * n_new, :] + res[off:off + n_new, :]
            cols = slice(h * hw, (h + 1) * hw)
            o_ref[0, :, cols] = _subln_gate(oh, g_ref[0, :, cols], sw_ref[...], lam_init)


def _sample_attention(page_table, lam, qbd, cache_k, cache_v, la, k_new, v_new, g,
                      bias_last, bias_new, far, subln_w, n_new, lam_init):
    bs, n_pages = page_table.shape
    n_layers, n_phys, page, n_maps, d = cache_k.shape
    n_heads = n_maps // 2
    width = n_maps * d
    ncol = qbd.shape[2]
    hw = 2 * ATTN_HEAD_DIM
    pad_new = 16
    assert ncol % 2 == 0 and page + 1 >= FAR_DISTANCE and n_pages % 2 == 0 and n_new <= pad_new <= page
    n_steps = n_pages // 2
    perm, expand, mask = _sample_attn_constants(page, n_maps, n_heads, n_new)
    ck = cache_k.reshape(n_layers, n_phys, page * n_maps, d)
    cv = cache_v.reshape(n_layers, n_phys, page * n_heads, hw)

    n_parts = SAMPLE_PAGE_PARTS
    assert page % (n_parts * SUBLANES) == 0

    def kmap(which, part):
        return lambda b, t, pt: (la, pt[b, 2 * jnp.minimum(t, n_steps - 1) + which], part, 0)

    def vmap(which, part):
        return lambda b, t, pt: (la, pt[b, 2 * jnp.maximum(t - n_steps, 0) + which], part, 0)

    kblock = (1, 1, ck.shape[2] // n_parts, d)
    vblock = (1, 1, cv.shape[2] // n_parts, hw)
    kv_specs = ([pl.BlockSpec(kblock, kmap(w, p)) for w in range(2) for p in range(n_parts)]
                + [pl.BlockSpec(vblock, vmap(w, p)) for w in range(2) for p in range(n_parts)])
    per_seq = lambda shape: pl.BlockSpec((1,) + shape, lambda b, t, pt: (b, 0, 0))
    const = lambda shape: pl.BlockSpec(shape, lambda b, t, pt: (0, 0))
    grid_spec = pltpu.PrefetchScalarGridSpec(
        num_scalar_prefetch=1,
        grid=(bs, 2 * n_steps),
        in_specs=[pl.BlockSpec(memory_space=pltpu.SMEM),
                  per_seq((width, ncol)), *kv_specs,
                  per_seq((n_new, width)), per_seq((n_new, width)), per_seq((n_new, width)),
                  const((page, ncol)), const((page, ncol)), const((1, ncol)), const((1, hw)),
                  const(perm.shape), const(expand.shape), const(mask.shape)],
        out_specs=per_seq((n_new, width)),
        scratch_shapes=[pltpu.VMEM((n_steps, 2 * page, ncol), F32), pltpu.VMEM((pad_new, ncol), F32),
                        pltpu.VMEM((1, ncol), F32), pltpu.VMEM((2, ncol // 2), F32),
                        pltpu.VMEM((n_heads * n_new, hw), F32), pltpu.VMEM((2 * page, width), BF16)],
    )
    return pl.pallas_call(
        functools.partial(_sample_attn_kernel, n_parts=n_parts, n_steps=n_steps, n_new=n_new,
                          scale=ATTN_HEAD_DIM ** -0.5, lam_init=lam_init),
        grid_spec=grid_spec,
        out_shape=jax.ShapeDtypeStruct((bs, n_new, width), F32),
        compiler_params=_cparams("parallel", "arbitrary"),
        name="sample_attention",
    )(page_table, lam, qbd, *([ck] * (2 * n_parts)), *([cv] * (2 * n_parts)), k_new, v_new, g,
      bias_last, bias_new, far, subln_w.reshape(1, hw), perm, expand, mask)


def _block_diag_queries(q, bs, n_new):
    d = ATTN_HEAD_DIM
    n_maps = q.shape[1] // d
    n_heads = n_maps // 2
    q5 = q.reshape(bs, n_new, n_heads, 2, d)
    qt = jnp.transpose(q5, (0, 2, 3, 4, 1))
    eye_h = jnp.eye(n_heads, dtype=q.dtype)
    eye_j = jnp.eye(2, dtype=q.dtype)
    out = (qt[:, :, :, :, None, None, :] * eye_j[None, None, :, None, :, None, None]
           * eye_h[None, :, None, None, None, :, None])
    return out.reshape(bs, n_maps * d, 2 * n_heads * n_new)


def _conv_kernel(x_ref, st_ref, w_ref, b_ref, y_ref, tail_ref, pad_ref, *, width):
    t = pl.program_id(2)
    tt = x_ref.shape[0]
    lo = 8 - (width - 1)

    @pl.when(t == 0)
    def _():
        pad_ref[lo:8, :] = st_ref[0]

    x = x_ref[...]
    pad_ref[8:8 + tt, :] = x
    acc = b_ref[...] + x * w_ref[width - 1:width, :]
    for j in range(width - 1):
        acc = acc + pad_ref[lo + j:lo + j + tt, :] * w_ref[j:j + 1, :]
    y_ref[...] = _silu(acc)
    tail = x[tt - (width - 1):, :]
    pad_ref[lo:8, :] = tail
    tail_ref[0] = tail


def _matmul_conv_kernel(a_ref, w_ref, cw_ref, cb_ref, y_ref, tail_ref, pad_ref, *, width, tiles_per_seq):
    i = pl.program_id(1)
    tm = a_ref.shape[0]
    lo = 8 - (width - 1)

    @pl.when(i % tiles_per_seq == 0)
    def _():
        pad_ref[lo:8, :] = jnp.zeros((width - 1, pad_ref.shape[1]), F32)

    chunk = min(2 * LANES, pad_ref.shape[1])
    for c0 in range(0, pad_ref.shape[1], chunk):
        cols = slice(c0, c0 + chunk)
        x = jnp.dot(a_ref[...], w_ref[:, cols], preferred_element_type=F32)
        pad_ref[8:8 + tm, cols] = x
        acc = cb_ref[:, cols] + x * cw_ref[width - 1:width, cols]
        for j in range(width - 1):
            acc = acc + pad_ref[lo + j:lo + j + tm, cols] * cw_ref[j:j + 1, cols]
        y_ref[:, cols] = _silu(acc)
        tail = x[tm - (width - 1):, :]
        pad_ref[lo:8, cols] = tail
        tail_ref[0, :, cols] = tail


def _matmul_conv(a, w, col0, conv_w, conv_b, batch, seq, tm, tn):
    m, k = a.shape
    width, c = conv_w.shape
    tm = min(tm, seq)
    assert seq % tm == 0 and c % tn == 0 and col0 % tn == 0 and tm >= width - 1
    cb0 = col0 // tn
    tiles_per_seq = seq // tm
    return pl.pallas_call(
        functools.partial(_matmul_conv_kernel, width=width, tiles_per_seq=tiles_per_seq),
        grid=(c // tn, m // tm),
        in_specs=[pl.BlockSpec((tm, k), lambda j, i: (i, 0)),
                  pl.BlockSpec((k, tn), lambda j, i: (0, cb0 + j)),
                  pl.BlockSpec((width, tn), lambda j, i: (0, j)),
                  pl.BlockSpec((1, tn), lambda j, i: (0, j))],
        out_specs=[pl.BlockSpec((tm, tn), lambda j, i: (i, j)),
                   pl.BlockSpec((1, width - 1, tn), lambda j, i: (i // tiles_per_seq, 0, j))],
        out_shape=[jax.ShapeDtypeStruct((m, c), F32),
                   jax.ShapeDtypeStruct((batch, width - 1, c), F32)],
        scratch_shapes=[pltpu.VMEM((8 + tm, tn), F32)],
        compiler_params=_cparams("parallel", "arbitrary"),
        name="matmul_conv",
    )(a, w, conv_w, conv_b.reshape(1, c))


def _conv(xbc, conv_state, conv_w, conv_b, batch, seq):
    m, c = xbc.shape
    width = conv_w.shape[0]
    tt = min(seq, 256)
    tc = 1024
    assert seq % tt == 0 and c % tc == 0 and tt >= width - 1
    nt = seq // tt
    return pl.pallas_call(
        functools.partial(_conv_kernel, width=width),
        grid=(batch, c // tc, nt),
        in_specs=[pl.BlockSpec((tt, tc), lambda b, j, t: (b * nt + t, j)),
                  pl.BlockSpec((1, width - 1, tc), lambda b, j, t: (b, 0, j)),
                  pl.BlockSpec((width, tc), lambda b, j, t: (0, j)),
                  pl.BlockSpec((1, tc), lambda b, j, t: (0, j))],
        out_specs=[pl.BlockSpec((tt, tc), lambda b, j, t: (b * nt + t, j)),
                   pl.BlockSpec((1, width - 1, tc), lambda b, j, t: (b, 0, j))],
        out_shape=[jax.ShapeDtypeStruct((m, c), F32),
                   jax.ShapeDtypeStruct((batch, width - 1, c), F32)],
        scratch_shapes=[pltpu.VMEM((8 + tt, tc), F32)],
        compiler_params=_cparams("parallel", "parallel", "arbitrary"),
        name="ssd_conv",
    )(xbc, conv_state, conv_w, conv_b.reshape(1, c))


def _split3(v):
    hi = v.astype(BF16)
    r1 = v - hi.astype(F32)
    mid = r1.astype(BF16)
    lo = (r1 - mid.astype(F32)).astype(BF16)
    return hi, mid, lo


def _dt_kernel(x_ref, bias_ref, alog_ref, dt_ref, cum_ref, dec_ref, *maybe_parts_ref):
    q = x_ref.shape[0]
    x = x_ref[...] + bias_ref[...]
    dt = jnp.maximum(x, 0.0) + jnp.log1p(jnp.exp(-jnp.abs(x)))
    c = dt * (-jnp.exp(alog_ref[...]))
    row = lax.broadcasted_iota(jnp.int32, c.shape, 0)
    shift = 1
    while shift < q:
        c = c + jnp.where(row >= shift, pltpu.roll(c, shift, axis=0), 0.0)
        shift *= 2
    dt_ref[...] = dt
    cum_ref[...] = c
    dec_ref[0] = jnp.exp(c[q - 1:q, :])
    if maybe_parts_ref:
        (parts_ref,) = maybe_parts_ref
        w_end = jnp.exp(c[q - 1:q, :] - c) * dt
        for i, v in enumerate((c, jnp.exp(c), w_end)):
            for j, part in enumerate(_split3(v)):
                parts_ref[3 * i + j] = part


def _dt_prep(dt_raw, dt_bias, a_log, q, with_parts):
    m, nh = dt_raw.shape
    row = pl.BlockSpec((q, nh), lambda i: (i, 0))
    vec = pl.BlockSpec((1, nh), lambda i: (0, 0))
    out_specs = [row, row, pl.BlockSpec((1, 1, nh), lambda i: (i, 0, 0))]
    out_shape = [jax.ShapeDtypeStruct((m, nh), F32), jax.ShapeDtypeStruct((m, nh), F32),
                 jax.ShapeDtypeStruct((m // q, 1, nh), F32)]
    if with_parts:
        out_specs.append(pl.BlockSpec((9, q, nh), lambda i: (0, i, 0)))
        out_shape.append(jax.ShapeDtypeStruct((9, m, nh), BF16))
    return pl.pallas_call(
        _dt_kernel,
        grid=(m // q,),
        in_specs=[row, vec, vec],
        out_specs=out_specs,
        out_shape=out_shape,
        compiler_params=_cparams("parallel"),
        name="ssd_dt_prep",
    )(dt_raw, dt_bias.reshape(1, nh), a_log.reshape(1, nh))


def _ssd_kernel(*refs, has_init, n_chunks):
    if has_init:
        (x_ref, b_ref, c_ref, z_ref, dt_ref, cum_ref, dtt_ref, cumt_ref, dec_ref, dskip_ref, gw_ref, s0_ref,
         y_ref, sout_ref, state_ref, ys_ref, xw_ref) = refs
    else:
        (x_ref, b_ref, c_ref, z_ref, dt_ref, cum_ref, dtt_ref, cumt_ref, dec_ref, dskip_ref, gw_ref,
         y_ref, sout_ref, state_ref, ys_ref, xw_ref) = refs
    ck = pl.program_id(2)
    q = x_ref.shape[0]
    r_heads = dt_ref.shape[3]
    p = SSM_HEAD_DIM

    @pl.when(ck == 0)
    def _():
        if has_init:
            state_ref[...] = s0_ref[0, 0].T
        else:
            state_ref[...] = jnp.zeros_like(state_ref)

    x = x_ref[...]
    xb = x.astype(BF16)
    bb = b_ref[...].astype(BF16)
    cb_ = c_ref[...].astype(BF16)
    cb = lax.dot_general(cb_, bb, (((1,), (1,)), ((), ())), preferred_element_type=F32)
    cs = jnp.dot(cb_, state_ref[...].astype(BF16), preferred_element_type=F32)
    dt = dt_ref[0, 0]
    cum = cum_ref[0, 0]
    dtt = dtt_ref[0, 0]
    cumt = cumt_ref[0, 0]
    w_end = jnp.exp(cum[q - 1:q, :] - cum) * dt
    ecum = jnp.exp(cum)
    li = lax.broadcasted_iota(jnp.int32, (q, q), 0)
    si = lax.broadcasted_iota(jnp.int32, (q, q), 1)
    causal = li >= si
    for r in range(r_heads):
        cols = slice(r * p, (r + 1) * p)
        seg = cum[:, r:r + 1] - cumt[r:r + 1, :]
        decay = jnp.exp(jnp.where(causal, seg, NEG_INF))
        mh = (cb * decay * dtt[r:r + 1, :]).astype(BF16)
        ys_ref[:, cols] = (jnp.dot(mh, xb[:, cols], preferred_element_type=F32)
                           + cs[:, cols] * ecum[:, r:r + 1])
        xw_ref[:, cols] = (x[:, cols] * w_end[:, r:r + 1]).astype(BF16)
    ds = lax.dot_general(bb, xw_ref[...], (((0,), (0,)), ((), ())), preferred_element_type=F32)
    state_ref[...] = state_ref[...] * dec_ref[0, 0, 0] + ds

    y = ys_ref[...] + x * dskip_ref[...]
    y = y * _silu(z_ref[...])
    y = y * lax.rsqrt(jnp.mean(y * y, axis=-1, keepdims=True) + NORM_EPS) * gw_ref[...]
    y_ref[...] = y.astype(y_ref.dtype)

    @pl.when(ck == n_chunks - 1)
    def _():
        sout_ref[0, 0] = state_ref[...].T


def _ssd_scan(act, z, dt, cum, dec_last, d_skip, gnorm_w, state0, batch, seq, q, out_dtype):
    m = act.shape[0]
    inner = z.shape[1]
    g = SSM_GROUPS
    n = SSM_STATE
    gw_cols = inner // g
    r = gw_cols // SSM_HEAD_DIM
    nc = seq // q
    assert n == LANES and gw_cols % LANES == 0
    dt4 = jnp.transpose(dt.reshape(batch, seq, g, r), (0, 2, 1, 3))
    cum4 = jnp.transpose(cum.reshape(batch, seq, g, r), (0, 2, 1, 3))
    dtt4 = jnp.transpose(dt4, (0, 1, 3, 2))
    cumt4 = jnp.transpose(cum4, (0, 1, 3, 2))
    dec5 = jnp.repeat(dec_last.reshape(batch, nc, g, 1, r), SSM_HEAD_DIM, axis=-1)
    dskip_row = jnp.repeat(d_skip, SSM_HEAD_DIM).reshape(1, inner)

    xcol0 = inner // gw_cols
    row_blk = lambda off: pl.BlockSpec((q, gw_cols), lambda b, gi, c: (b * nc + c, off + gi))
    bspec = pl.BlockSpec((q, n), lambda b, gi, c: (b * nc + c, inner // n + gi))
    cspec = pl.BlockSpec((q, n), lambda b, gi, c: (b * nc + c, inner // n + g + gi))
    tcol = pl.BlockSpec((1, 1, q, r), lambda b, gi, c: (b, gi, c, 0))
    trow = pl.BlockSpec((1, 1, r, q), lambda b, gi, c: (b, gi, 0, c))
    vec = pl.BlockSpec((1, gw_cols), lambda b, gi, c: (0, gi))
    sspec = pl.BlockSpec((1, 1, gw_cols, n), lambda b, gi, c: (b, gi, 0, 0))
    in_specs = [row_blk(0), bspec, cspec, row_blk(0), tcol, tcol, trow, trow,
                pl.BlockSpec((1, 1, 1, 1, gw_cols), lambda b, gi, c: (b, c, gi, 0, 0)), vec, vec]
    args = [act, act, act, z, dt4, cum4, dtt4, cumt4, dec5, dskip_row, gnorm_w.reshape(1, inner)]
    if state0 is not None:
        in_specs.append(sspec)
        args.append(state0.reshape(batch, g, gw_cols, n))
    del xcol0
    y, s_out = pl.pallas_call(
        functools.partial(_ssd_kernel, has_init=state0 is not None, n_chunks=nc),
        grid=(batch, g, nc),
        in_specs=in_specs,
        out_specs=[row_blk(0), sspec],
        out_shape=[jax.ShapeDtypeStruct((m, inner), out_dtype),
                   jax.ShapeDtypeStruct((batch, g, gw_cols, n), F32)],
        scratch_shapes=[pltpu.VMEM((n, gw_cols), F32), pltpu.VMEM((q, gw_cols), F32),
                        pltpu.VMEM((q, gw_cols), BF16)],
        compiler_params=_cparams("parallel", "parallel", "arbitrary"),
        name="ssd_scan_stateful",
    )(*args)
    return y, s_out


def _expansion_matrix(r_heads):
    p = SSM_HEAD_DIM
    e = np.zeros((9 * r_heads, r_heads * (LANES + 2 * p)), np.float32)
    for k in range(9):
        for r in range(r_heads):
            if k < 3:
                lo, w = r * LANES, LANES
            else:
                lo, w = r_heads * LANES + (k // 3 - 1) * r_heads * p + r * p, p
            e[k * r_heads + r, lo:lo + w] = 1.0
    return jnp.asarray(e, BF16)


def _ssd_chunk_kernel(x_ref, b_ref, c_ref, z_ref, parts_ref, dtt_ref, cumt_ref, e_ref, dec_ref, dskip_ref, gw_ref,
                      y_ref, sout_ref, state_ref, xl_ref, ys_ref, *, n_chunks):
    ck = pl.program_id(2)
    q = x_ref.shape[0]
    r_heads = dtt_ref.shape[2]
    p = SSM_HEAD_DIM

    @pl.when(ck == 0)
    def _():
        state_ref[...] = jnp.zeros_like(state_ref)

    x = x_ref[...]
    bb = b_ref[...].astype(BF16)
    cb_ = c_ref[...].astype(BF16)
    cb = lax.dot_general(cb_, bb, (((1,), (1,)), ((), ())), preferred_element_type=F32)
    cs = jnp.dot(cb_, state_ref[...].astype(BF16), preferred_element_type=F32)
    xl_ref[...] = jnp.dot(parts_ref[0, 0], e_ref[...], preferred_element_type=F32)
    dtt = dtt_ref[0, 0]
    cumt = cumt_ref[0, 0]
    li = lax.broadcasted_iota(jnp.int32, (q, q), 0)
    si = lax.broadcasted_iota(jnp.int32, (q, q), 1)
    causal = li >= si
    first = lax.broadcasted_iota(jnp.int32, (q, 2 * p), 1) < p
    for pair in range(r_heads // 2):
        mhs = []
        for r in (2 * pair, 2 * pair + 1):
            seg = xl_ref[:, r * LANES:(r + 1) * LANES] - cumt[r:r + 1, :]
            decay = jnp.exp(jnp.where(causal, seg, NEG_INF))
            mhs.append((cb * decay * dtt[r:r + 1, :]).astype(BF16))
        xp = x[:, pair * 2 * p:(pair + 1) * 2 * p]
        rhs = jnp.concatenate([jnp.where(first, xp, 0.0).astype(BF16),
                               jnp.where(first, 0.0, xp).astype(BF16)], axis=0)
        ys_ref[:, pair * 2 * p:(pair + 1) * 2 * p] = jnp.dot(jnp.concatenate(mhs, axis=1), rhs,
                                                             preferred_element_type=F32)
    c0 = r_heads * LANES
    ecum = xl_ref[:, c0:c0 + r_heads * p]
    w_end = xl_ref[:, c0 + r_heads * p:c0 + 2 * r_heads * p]
    xw = (x * w_end).astype(BF16)
    ds = lax.dot_general(bb, xw, (((0,), (0,)), ((), ())), preferred_element_type=F32)
    state_ref[...] = state_ref[...] * dec_ref[0, 0, 0] + ds

    y = ys_ref[...] + cs * ecum + x * dskip_ref[...]
    y = y * _silu(z_ref[...])
    y = y * lax.rsqrt(jnp.mean(y * y, axis=-1, keepdims=True) + NORM_EPS) * gw_ref[...]
    y_ref[...] = y.astype(y_ref.dtype)

    @pl.when(ck == n_chunks - 1)
    def _():
        sout_ref[0, 0] = state_ref[...].T


def _ssd_scan_chunked(act, z, dt, cum, dec_last, parts, d_skip, gnorm_w, batch, seq, q, out_dtype):
    m = act.shape[0]
    inner = z.shape[1]
    g = SSM_GROUPS
    n = SSM_STATE
    gw_cols = inner // g
    r = gw_cols // SSM_HEAD_DIM
    nc = seq // q
    assert n == LANES and q == LANES and gw_cols % LANES == 0 and r % 2 == 0
    dtt4 = jnp.transpose(dt.reshape(batch, seq, g, r), (0, 2, 3, 1))
    cumt4 = jnp.transpose(cum.reshape(batch, seq, g, r), (0, 2, 3, 1))
    parts4 = jnp.transpose(parts.reshape(9, batch, seq, g, r), (1, 3, 2, 0, 4)).reshape(batch, g, seq, 9 * r)
    dec5 = jnp.repeat(dec_last.reshape(batch, nc, g, 1, r), SSM_HEAD_DIM, axis=-1)
    dskip_row = jnp.repeat(d_skip, SSM_HEAD_DIM).reshape(1, inner)
    emat = _expansion_matrix(r)

    row_blk = pl.BlockSpec((q, gw_cols), lambda b, gi, c: (b * nc + c, gi))
    bspec = pl.BlockSpec((q, n), lambda b, gi, c: (b * nc + c, inner // n + gi))
    cspec = pl.BlockSpec((q, n), lambda b, gi, c: (b * nc + c, inner // n + g + gi))
    trow = pl.BlockSpec((1, 1, r, q), lambda b, gi, c: (b, gi, 0, c))
    vec = pl.BlockSpec((1, gw_cols), lambda b, gi, c: (0, gi))
    sspec = pl.BlockSpec((1, 1, gw_cols, n), lambda b, gi, c: (b, gi, 0, 0))
    y, s_out = pl.pallas_call(
        functools.partial(_ssd_chunk_kernel, n_chunks=nc),
        grid=(batch, g, nc),
        in_specs=[row_blk, bspec, cspec, row_blk,
                  pl.BlockSpec((1, 1, q, 9 * r), lambda b, gi, c: (b, gi, c, 0)), trow, trow,
                  pl.BlockSpec(emat.shape, lambda b, gi, c: (0, 0)),
                  pl.BlockSpec((1, 1, 1, 1, gw_cols), lambda b, gi, c: (b, c, gi, 0, 0)), vec, vec],
        out_specs=[row_blk, sspec],
        out_shape=[jax.ShapeDtypeStruct((m, inner), out_dtype),
                   jax.ShapeDtypeStruct((batch, g, gw_cols, n), F32)],
        scratch_shapes=[pltpu.VMEM((n, gw_cols), F32), pltpu.VMEM((q, emat.shape[1]), F32),
                        pltpu.VMEM((q, gw_cols), F32)],
        compiler_params=_cparams("parallel", "parallel", "arbitrary"),
        name="ssd_scan_chunked",
    )(act, act, act, z, parts4, dtt4, cumt4, emat, dec5, dskip_row, gnorm_w.reshape(1, inner))
    return y, s_out


def _attn_layer(hp, hs, w_in, w_out, lambda_qk_l, subln_w_l, rel_bias, cache_k, cache_v, la, page_table,
                batch, seq, bs, n_new, lam_init):
    width = w_out.shape[0]
    tile = min(seq, 512)
    assert seq % tile == 0 and tile + 1 >= FAR_DISTANCE
    page = cache_k.shape[2]

    bias_tiles = _bias_tiles(rel_bias, tile)
    bias_last, bias_new, far, lam = _sample_prep(rel_bias, lambda_qk_l, page, n_new, lam_init)

    (q,) = _matmul(hp, w_in, 0, width, [BF16], 512, 1024)
    k, kb = _matmul(hp, w_in, width, width, [F32, BF16], 512, 1024)
    v, vb = _matmul(hp, w_in, 2 * width, width, [F32, BF16], 512, 1024)
    (g,) = _matmul(hp, w_in, 3 * width, width, [F32], 512, 1024)
    og = _prompt_attention(lam, rel_bias, q, kb, vb, g, bias_tiles, subln_w_l, batch, seq, tile, lam_init)
    (op,) = _matmul(og, w_out, 0, w_out.shape[1], [F32], 512, 1024)

    (q2,) = _matmul(hs, w_in, 0, width, [BF16], 512, 1024)
    (k2,) = _matmul(hs, w_in, width, width, [F32], 512, 1024)
    (v2,) = _matmul(hs, w_in, 2 * width, width, [F32], 512, 1024)
    (g2,) = _matmul(hs, w_in, 3 * width, width, [F32], 512, 1024)
    qbd = _block_diag_queries(q2, bs, n_new)
    og2 = _sample_attention(page_table, lam, qbd, cache_k, cache_v, la,
                            k2.reshape(bs, n_new, width), v2.reshape(bs, n_new, width),
                            g2.reshape(bs, n_new, width), bias_last, bias_new, far, subln_w_l, n_new, lam_init)
    (os_,) = _matmul(og2.reshape(bs * n_new, width).astype(BF16), w_out, 0, w_out.shape[1], [F32], 512, 1024)
    return op, os_, k, v, k2, v2


def _ssd_layer(h, conv_state, ssm_state, w_in, conv_w, conv_b, dt_bias, a_log, d_skip, gnorm_w, w_out,
               batch, seq):
    inner = w_out.shape[0]
    conv_dim = conv_w.shape[1]
    n_heads = dt_bias.shape[0]
    q = SSD_CHUNK if seq % SSD_CHUNK == 0 else seq
    (z,) = _matmul(h, w_in, 0, inner, [F32], 512, 1024)
    (dt_raw,) = _matmul(h, w_in, inner + conv_dim, n_heads, [F32], 512, n_heads)
    if conv_state is None:
        act, conv_out = _matmul_conv(h, w_in, inner, conv_w, conv_b, batch, seq, 512, 1024)
    else:
        (xbc,) = _matmul(h, w_in, inner, conv_dim, [F32], 512, 1024)
        act, conv_out = _conv(xbc, conv_state, conv_w, conv_b, batch, seq)
    if ssm_state is None and q == LANES:
        dt, cum, dec_last, parts = _dt_prep(dt_raw, dt_bias, a_log, q, True)
        y, s_out = _ssd_scan_chunked(act, z, dt, cum, dec_last, parts, d_skip, gnorm_w, batch, seq, q, BF16)
    else:
        dt, cum, dec_last = _dt_prep(dt_raw, dt_bias, a_log, q, False)
        y, s_out = _ssd_scan(act, z, dt, cum, dec_last, d_skip, gnorm_w, ssm_state, batch, seq, q, F32)
    (o,) = _matmul(y.astype(BF16), w_out, 0, w_out.shape[1], [F32], 512, 512)
    return o, conv_out, s_out.reshape(batch, n_heads, SSM_HEAD_DIM, SSM_STATE)


def kernel(x_prompt, x_sample, cache_k, cache_v, page_table, state_conv, state_ssm, norm_pre, norm_post, rel_bias,
           w_attn_in, lambda_qk, subln_w, w_attn_out, w_ssm_in, conv_w, conv_b, dt_bias, a_log, d_skip, gnorm_w,
           w_ssm_out):
    bp, sp, dm = x_prompt.shape
    bs, ss, _ = x_sample.shape
    depth = norm_pre.shape[0]
    xp = x_prompt.reshape(bp * sp, dm)
    xs = x_sample.reshape(bs * ss, dm)
    kp_l, vp_l, ks_l, vs_l, cp_l, sp_l, cs_l, ss_l = [], [], [], [], [], [], [], []
    for i in range(depth):
        hp = _rmsnorm(xp, norm_pre[i], NORM_EPS, BF16)
        hs = _rmsnorm(xs, norm_pre[i], NORM_EPS, BF16)
        if i % N_MIXERS == 0:
            la = i // N_MIXERS
            lam_init = 0.8 - 0.6 * math.exp(-0.3 * i)
            op, os_, k, v, k2, v2 = _attn_layer(
                hp, hs, w_attn_in[la].astype(BF16), w_attn_out[la].astype(BF16), lambda_qk[la], subln_w[la],
                rel_bias, cache_k, cache_v, la, page_table, bp, sp, bs, ss, lam_init)
            n_maps = k.shape[1] // ATTN_HEAD_DIM
            kp_l.append(k.reshape(bp, sp, n_maps, ATTN_HEAD_DIM))
            vp_l.append(v.reshape(bp, sp, n_maps // 2, 2 * ATTN_HEAD_DIM))
            ks_l.append(k2.reshape(bs, ss, n_maps, ATTN_HEAD_DIM))
            vs_l.append(v2.reshape(bs, ss, n_maps // 2, 2 * ATTN_HEAD_DIM))
        else:
            ls = i // N_MIXERS
            wts = (w_ssm_in[ls].astype(BF16), conv_w[ls], conv_b[ls], dt_bias[ls], a_log[ls], d_skip[ls],
                   gnorm_w[ls], w_ssm_out[ls].astype(BF16))
            op, cpn, spn = _ssd_layer(hp, None, None, *wts, bp, sp)
            os_, csn, ssn = _ssd_layer(hs, state_conv[ls], state_ssm[ls], *wts, bs, ss)
            cp_l.append(cpn)
            sp_l.append(spn)
            cs_l.append(csn)
            ss_l.append(ssn)
        xp = _residual_norm(xp, op, norm_post[i], NORM_EPS)
        xs = _residual_norm(xs, os_, norm_post[i], NORM_EPS)
    return (xp.reshape(bp, sp, dm), xs.reshape(bs, ss, dm),
            jnp.stack(kp_l), jnp.stack(vp_l), jnp.stack(ks_l), jnp.stack(vs_l),
            jnp.stack(cp_l), jnp.stack(sp_l), jnp.stack(cs_l), jnp.stack(ss_l))
```

```python
import functools
import math

import numpy as np
import jax
import jax.numpy as jnp
from jax import lax
from jax.experimental import pallas as pl
from jax.experimental.pallas import tpu as pltpu

ATTN_HEAD_DIM = 128
N_BUCKETS = 32
MAX_DISTANCE = 128
SUBLN_EPS = 1e-5
NORM_EPS = 1e-6
SSM_HEAD_DIM = 64
SSM_GROUPS = 8
SSM_STATE = 128
CONV_WIDTH = 4
SSD_CHUNK = 128
N_MIXERS = 2

V7X_VMEM_LIMIT_BYTES = 56 * 1024 * 1024
V7X_FUSED_ATTN_VMEM_LIMIT_BYTES = 60 * 1024 * 1024
LANES = 128

F32 = jnp.float32
BF16 = jnp.bfloat16
NEG_INF = float("-inf")


def _cparams(*sem):
    return pltpu.CompilerParams(dimension_semantics=sem, vmem_limit_bytes=V7X_VMEM_LIMIT_BYTES)


def _bucket_thresholds():
    n = np.arange(0, 4 * MAX_DISTANCE)
    max_exact = N_BUCKETS // 2
    nf = np.maximum(n, max_exact).astype(np.float32)
    large = max_exact + (np.log(nf / max_exact) / math.log(MAX_DISTANCE / max_exact)
                         * (N_BUCKETS - max_exact)).astype(np.int32)
    bucket = np.where(n < max_exact, n, np.minimum(large, N_BUCKETS - 1))
    return [int(np.argmax(bucket >= b)) for b in range(N_BUCKETS)]


BUCKET_THR = _bucket_thresholds()
FAR_DISTANCE = BUCKET_THR[N_BUCKETS - 1]


def _bias_of_distance(n, rel_of_bucket):
    v = jnp.broadcast_to(rel_of_bucket(0), n.shape).astype(F32)
    for b in range(1, N_BUCKETS):
        v = jnp.where(n >= BUCKET_THR[b], rel_of_bucket(b), v)
    return v


def _bias_tiles_kernel(rel_ref, out_ref, *, tile):
    h = pl.program_id(0)
    i = lax.broadcasted_iota(jnp.int32, (tile, tile), 0)
    j = lax.broadcasted_iota(jnp.int32, (tile, tile), 1)
    d = i - j
    rel = lambda b: rel_ref[b, h]
    out_ref[0, 0] = jnp.where(d >= 0, _bias_of_distance(d, rel), NEG_INF)
    out_ref[0, 1] = _bias_of_distance(d + tile, rel)


def _bias_tiles(rel_bias, tile):
    n_heads = rel_bias.shape[1]
    return pl.pallas_call(
        functools.partial(_bias_tiles_kernel, tile=tile),
        grid=(n_heads,),
        in_specs=[pl.BlockSpec(memory_space=pltpu.SMEM)],
        out_specs=pl.BlockSpec((1, 2, tile, tile), lambda h: (h, 0, 0, 0)),
        out_shape=jax.ShapeDtypeStruct((n_heads, 2, tile, tile), F32),
        compiler_params=_cparams("arbitrary"),
        name="bias_tiles",
    )(rel_bias)


def _sample_prep_kernel(relc_ref, lq_ref, bias_last_ref, bias_new_ref, far_ref, lam_ref, *, page, n_new, lam_init):
    ncol = relc_ref.shape[1]
    key = lax.broadcasted_iota(jnp.int32, (page, ncol), 0)
    col = lax.broadcasted_iota(jnp.int32, (page, ncol), 1)
    qi = col % n_new
    rel = lambda b: relc_ref[b:b + 1, :]
    bias_last_ref[...] = _bias_of_distance(page + qi - key, rel)
    d = qi - key
    bias_new_ref[...] = jnp.where(d >= 0, _bias_of_distance(d, rel), NEG_INF)
    far_ref[...] = relc_ref[N_BUCKETS - 1:N_BUCKETS, :]
    lq = lq_ref[...]
    lam = (jnp.exp(jnp.sum(lq[0:1] * lq[1:2], axis=-1, keepdims=True))
           - jnp.exp(jnp.sum(lq[2:3] * lq[3:4], axis=-1, keepdims=True)) + lam_init)
    lam_ref[...] = lam


def _sample_prep(rel_bias, lambda_qk_l, page, n_new, lam_init):
    n_heads = rel_bias.shape[1]
    ncol = 2 * n_heads * n_new
    relc = jnp.tile(jnp.repeat(rel_bias, n_new, axis=1), (1, 2))
    full = lambda shape: pl.BlockSpec(shape, lambda: (0,) * len(shape))
    return pl.pallas_call(
        functools.partial(_sample_prep_kernel, page=page, n_new=n_new, lam_init=lam_init),
        in_specs=[full(relc.shape), full(lambda_qk_l.shape)],
        out_specs=[full((page, ncol)), full((page, ncol)), full((1, ncol)), full((1, 1))],
        out_shape=[jax.ShapeDtypeStruct((page, ncol), F32), jax.ShapeDtypeStruct((page, ncol), F32),
                   jax.ShapeDtypeStruct((1, ncol), F32), jax.ShapeDtypeStruct((1, 1), F32)],
    )(relc, lambda_qk_l)


def _rmsnorm_kernel(x_ref, w_ref, o_ref, *, eps):
    x = x_ref[...]
    y = x * lax.rsqrt(jnp.mean(x * x, axis=-1, keepdims=True) + eps)
    o_ref[...] = (y * w_ref[...]).astype(o_ref.dtype)


def _rmsnorm(x2d, w, eps, out_dtype):
    m, d = x2d.shape
    tm = min(m, 256)
    return pl.pallas_call(
        functools.partial(_rmsnorm_kernel, eps=eps),
        grid=(m // tm,),
        in_specs=[pl.BlockSpec((tm, d), lambda i: (i, 0)), pl.BlockSpec((1, d), lambda i: (0, 0))],
        out_specs=pl.BlockSpec((tm, d), lambda i: (i, 0)),
        out_shape=jax.ShapeDtypeStruct((m, d), out_dtype),
        compiler_params=_cparams("parallel"),
        name="rmsnorm",
    )(x2d, w.reshape(1, d))


def _residual_norm_kernel(x_ref, y_ref, w_ref, o_ref, *, eps):
    y = y_ref[...]
    yn = y * lax.rsqrt(jnp.mean(y * y, axis=-1, keepdims=True) + eps)
    o_ref[...] = x_ref[...] + yn * w_ref[...]


def _residual_norm(x2d, y2d, w, eps):
    m, d = x2d.shape
    tm = min(m, 256)
    row = pl.BlockSpec((tm, d), lambda i: (i, 0))
    return pl.pallas_call(
        functools.partial(_residual_norm_kernel, eps=eps),
        grid=(m // tm,),
        in_specs=[row, row, pl.BlockSpec((1, d), lambda i: (0, 0))],
        out_specs=row,
        out_shape=jax.ShapeDtypeStruct((m, d), F32),
        compiler_params=_cparams("parallel"),
        name="residual_norm",
    )(x2d, y2d, w.reshape(1, d))


def _matmul_kernel(a_ref, w_ref, *o_refs):
    acc = jnp.dot(a_ref[...], w_ref[...], preferred_element_type=F32)
    for o_ref in o_refs:
        o_ref[...] = acc.astype(o_ref.dtype)


def _matmul(a, w, col0, n, out_dtypes, tm, tn):
    m, k = a.shape
    tm = min(tm, m)
    tn = min(tn, n)
    assert m % tm == 0 and n % tn == 0 and col0 % tn == 0
    cb0 = col0 // tn
    outs = pl.pallas_call(
        _matmul_kernel,
        grid=(n // tn, m // tm),
        in_specs=[pl.BlockSpec((tm, k), lambda j, i: (i, 0)),
                  pl.BlockSpec((k, tn), lambda j, i: (0, cb0 + j))],
        out_specs=[pl.BlockSpec((tm, tn), lambda j, i: (i, j)) for _ in out_dtypes],
        out_shape=[jax.ShapeDtypeStruct((m, n), dt) for dt in out_dtypes],
        compiler_params=_cparams("parallel", "parallel"),
        name="matmul",
    )(a, w)
    return outs


def _silu(g):
    return g * (1.0 / (1.0 + jnp.exp(-g)))


def _subln_gate(o, g, sw, lam_init):
    o = o * lax.rsqrt(jnp.mean(o * o, axis=-1, keepdims=True) + SUBLN_EPS) * sw
    return o * (1.0 - lam_init) * _silu(g)


def _prompt_attn_unit(unit, lam_ref, rel_ref, q_ref, k_ref, v_ref, g_ref, bias_ref, sw_ref, o_ref,
                      m_ref, l_ref, acc_ref, *, tile, scale, lam_init):
    h, kb, kind, last = unit
    d = ATTN_HEAD_DIM
    far = rel_ref[N_BUCKETS - 1, h]
    row0 = pl.multiple_of(kb * tile, tile)

    def scores(j, kblk, bias):
        s = lax.dot_general(q_ref[:, j * d:(j + 1) * d], kblk[:, j * d:(j + 1) * d],
                            (((1,), (1,)), ((), ())), preferred_element_type=F32)
        return s * scale + bias

    def row_reduce(x, fold, reduce):
        acc = x[:, 0:LANES]
        for i in range(1, x.shape[1] // LANES):
            acc = fold(acc, x[:, i * LANES:(i + 1) * LANES])
        return reduce(acc, axis=-1, keepdims=True)

    def block(bias, first):
        kblk = k_ref[pl.ds(row0, tile), :]
        vblk = v_ref[pl.ds(row0, tile), :]
        for j in range(2):
            s = scores(j, kblk, bias)
            row_max = row_reduce(s, jnp.maximum, jnp.max)
            m = row_max if first else jnp.maximum(m_ref[j], row_max)
            p = jnp.exp(s - m)
            row_sum = row_reduce(p, jnp.add, jnp.sum)
            pv = jnp.dot(p.astype(BF16), vblk, preferred_element_type=F32)
            if first:
                l_ref[j] = row_sum
                acc_ref[j] = pv
            else:
                alpha = jnp.exp(m_ref[j] - m)
                l_ref[j] = alpha * l_ref[j] + row_sum
                acc_ref[j] = alpha * acc_ref[j] + pv
            m_ref[j] = m

    @pl.when(kind == 0)
    def _():
        block(bias_ref[0, 0], True)

    @pl.when(kind == 1)
    def _():
        block(bias_ref[0, 1], False)

    @pl.when(kind == 2)
    def _():
        block(far, False)

    @pl.when(last == 1)
    def _():
        lam = lam_ref[0, 0]
        o = acc_ref[0] / l_ref[0] - lam * (acc_ref[1] / l_ref[1])
        o_ref[...] = _subln_gate(o, g_ref[...], sw_ref[...], lam_init).astype(o_ref.dtype)


def _prompt_attn_schedule(batch, n_heads, nq, n_cycles, cycle_steps):
    units = []
    for b in range(batch):
        for h in range(n_heads):
            for qb in range(nq):
                kbs = [(qb, 0)] + ([(qb - 1, 1)] if qb >= 1 else []) + [(kb, 2) for kb in range(qb - 1)]
                for i, (kb, kind) in enumerate(kbs):
                    units.append((1, b, h, qb, kb, kind, int(i == len(kbs) - 1)))
    per_cycle = -(-len(units) // n_cycles)
    assert per_cycle <= cycle_steps
    tab = np.zeros((7, n_cycles * cycle_steps), np.int32)
    current = units[0]
    for c in range(n_cycles):
        mine = units[c * per_cycle:(c + 1) * per_cycle]
        steps = sorted(range(cycle_steps - 1, -1, -1)[:len(mine)])
        by_step = dict(zip(steps, mine))
        for t in range(cycle_steps):
            if t in by_step:
                current = by_step[t]
                tab[:, c * cycle_steps + t] = current
            else:
                tab[:, c * cycle_steps + t] = (0,) + current[1:4] + (0, 0, 0)
    return tab


SUBLANES = 8
SAMPLE_PAGE_PARTS = 1


def _sample_attn_constants(page, n_maps, n_heads, n_new):
    rb = SUBLANES * n_maps
    perm = np.zeros((rb, rb), np.float32)
    for c in range(n_maps):
        for k8 in range(SUBLANES):
            perm[c * SUBLANES + k8, k8 * n_maps + c] = 1.0
    expand = np.zeros((page, page * n_heads), np.float32)
    for key in range(page):
        expand[key, key * n_heads:(key + 1) * n_heads] = 1.0
    row_head = np.arange(n_heads * n_new)[:, None] // n_new
    col_head = np.arange(page * n_heads)[None, :] % n_heads
    mask = (row_head == col_head).astype(np.float32)
    return jnp.asarray(perm, BF16), jnp.asarray(expand, BF16), jnp.asarray(mask, F32)


def _sample_attn_kernel(pt_ref, lam_ref, qbd_ref, *refs, n_parts, n_steps, n_new, scale, lam_init):
    ka_refs, kb_refs = refs[0:n_parts], refs[n_parts:2 * n_parts]
    va_refs, vb_refs = refs[2 * n_parts:3 * n_parts], refs[3 * n_parts:4 * n_parts]
    (kn_ref, vn_ref, g_ref, bias_last_ref, bias_new_ref, far_ref, sw_ref, perm_ref, expand_ref, mask_ref, o_ref,
     s_ref, sn_ref, m_ref, w_ref, acc_ref, kscr_ref) = refs[4 * n_parts:]
    t = pl.program_id(1)
    d = ATTN_HEAD_DIM
    rb = perm_ref.shape[0]
    n_maps = rb // SUBLANES
    page = n_parts * ka_refs[0].shape[2] // n_maps

    def page_rows(part_refs):
        return jnp.concatenate([r[0, 0] for r in part_refs], axis=0)

    n_blocks = page // SUBLANES
    half = s_ref.shape[2] // 2
    hw = 2 * d
    n_heads = n_maps // 2
    pad_new = sn_ref.shape[0]

    def gather_keys(k_refs, row0):
        kb16 = page_rows(k_refs).astype(BF16)
        w = jnp.concatenate([kb16[b * rb:(b + 1) * rb] for b in range(n_blocks)], axis=1)
        out = jnp.dot(perm_ref[...], w, preferred_element_type=F32)

        def block_rows(b):
            return jnp.concatenate([out[c * SUBLANES:(c + 1) * SUBLANES, b * d:(b + 1) * d] for c in range(n_maps)],
                                   axis=1)

        for b in range(0, n_blocks, 2):
            r0 = row0 + b * SUBLANES
            kscr_ref[r0:r0 + 2 * SUBLANES, :] = jnp.concatenate([block_rows(b), block_rows(b + 1)],
                                                                 axis=0).astype(BF16)

    @pl.when(t < n_steps)
    def _():
        gather_keys(ka_refs, 0)
        gather_keys(kb_refs, page)
        s = jnp.dot(kscr_ref[...], qbd_ref[0], preferred_element_type=F32) * scale
        s_ref[t, 0:page, :] = s[0:page] + far_ref[...]
        s_ref[t, page:2 * page, :] = s[page:] + jnp.where(t == n_steps - 1, bias_last_ref[...], far_ref[...])
        pmax = jnp.max(s_ref[t], axis=0, keepdims=True)

        @pl.when(t == 0)
        def _():
            m_ref[...] = pmax

        @pl.when(t > 0)
        def _():
            m_ref[...] = jnp.maximum(m_ref[...], pmax)

    @pl.when(t == n_steps - 1)
    def _():
        kn = jnp.concatenate([kn_ref[0], jnp.zeros((pad_new - n_new, kn_ref.shape[2]), F32)], axis=0)
        sn = (jnp.dot(kn.astype(BF16), qbd_ref[0], preferred_element_type=F32) * scale
              + bias_new_ref[0:pad_new, :])
        m = jnp.maximum(m_ref[...], jnp.max(sn, axis=0, keepdims=True))
        pn = jnp.exp(sn - m)
        sn_ref[...] = pn

        def body(slot, l):
            p = jnp.exp(s_ref[slot] - m)
            s_ref[slot] = p
            return l + jnp.sum(p, axis=0, keepdims=True)

        l = lax.fori_loop(0, n_steps, body, jnp.sum(pn, axis=0, keepdims=True))
        inv = 1.0 / l
        w_ref[0:1, :] = inv[:, :half]
        w_ref[1:2, :] = lam_ref[0, 0] * inv[:, half:]
        acc_ref[...] = jnp.zeros_like(acc_ref)

    def combined_probs_t(p):
        a = p[:, :half] * w_ref[0:1, :] - p[:, half:] * w_ref[1:2, :]
        return a.T.astype(BF16)

    def accumulate_page(p, v_refs):
        at = combined_probs_t(p)
        spread = jnp.dot(at, expand_ref[...], preferred_element_type=F32) * mask_ref[...]
        acc_ref[...] += jnp.dot(spread.astype(BF16), page_rows(v_refs).astype(BF16), preferred_element_type=F32)

    @pl.when(t >= n_steps)
    def _():
        p = s_ref[t - n_steps]
        accumulate_page(p[0:page], va_refs)
        accumulate_page(p[page:], vb_refs)

    @pl.when(t == 2 * n_steps - 1)
    def _():
        pn = jnp.concatenate([sn_ref[...], jnp.zeros((page - pad_new, 2 * half), F32)], axis=0)
        at = combined_probs_t(pn)
        rows = 2 * n_new
        for h in range(n_heads):
            vh = jnp.concatenate([vn_ref[0, :, h * hw:(h + 1) * hw], jnp.zeros((page - n_new, hw), F32)], axis=0)
            r0 = (h * n_new) // rows * rows
            res = jnp.dot(at[r0:r0 + rows, :], vh.astype(BF16), preferred_element_type=F32)
            off = h * n_new - r0
            oh = acc_ref[h * n_new:(h + 1) * n_new, :] + res[off:off + n_new, :]
            cols = slice(h * hw, (h + 1) * hw)
            o_ref[0, :, cols] = _subln_gate(oh, g_ref[0, :, cols], sw_ref[...], lam_init)


def _fused_attn_kernel(pt_ref, tab_ref, lam_ref, rel_ref, qbd_ref, *refs, n_parts, n_steps, n_new, tile, scale,
                       lam_init):
    n_kv = 4 * n_parts
    kv_refs = refs[:n_kv]
    (kn_ref, vn_ref, gs_ref, bias_last_ref, bias_new_ref, far_ref, sw_ref, perm_ref, expand_ref, mask_ref,
     q_ref, k_ref, v_ref, gp_ref, bias_ref, os_ref, op_ref,
     s_ref, sn_ref, ms_ref, w_ref, accs_ref, kscr_ref, mp_ref, lp_ref, accp_ref) = refs[n_kv:]
    _sample_attn_kernel(pt_ref, lam_ref, qbd_ref, *kv_refs, kn_ref, vn_ref, gs_ref, bias_last_ref, bias_new_ref,
                        far_ref, sw_ref, perm_ref, expand_ref, mask_ref, os_ref,
                        s_ref, sn_ref, ms_ref, w_ref, accs_ref, kscr_ref,
                        n_parts=n_parts, n_steps=n_steps, n_new=n_new, scale=scale, lam_init=lam_init)
    step = pl.program_id(0) * (2 * n_steps) + pl.program_id(1)

    @pl.when(tab_ref[0, step] == 1)
    def _():
        unit = (tab_ref[2, step], tab_ref[4, step], tab_ref[5, step], tab_ref[6, step])
        _prompt_attn_unit(unit, lam_ref, rel_ref, q_ref, k_ref, v_ref, gp_ref, bias_ref, sw_ref, op_ref,
                          mp_ref, lp_ref, accp_ref, tile=tile, scale=scale, lam_init=lam_init)


def _attention(page_table, lam, rel_bias, qbd, cache_k, cache_v, la, k_new, v_new, g_new,
               bias_last, bias_new, far, subln_w, q, k, v, g, bias_tiles, batch, seq, tile, n_new, lam_init):
    bs, n_pages = page_table.shape
    n_layers, n_phys, page, n_maps, d = cache_k.shape
    n_heads = n_maps // 2
    width = n_maps * d
    ncol = qbd.shape[2]
    hw = 2 * ATTN_HEAD_DIM
    nq = seq // tile
    pad_new = 16
    assert ncol % 2 == 0 and page + 1 >= FAR_DISTANCE and n_pages % 2 == 0 and n_new <= pad_new <= page
    n_steps = n_pages // 2
    cycle = 2 * n_steps
    perm, expand, mask = _sample_attn_constants(page, n_maps, n_heads, n_new)
    table = jnp.asarray(_prompt_attn_schedule(batch, n_heads, nq, bs, cycle))
    ck = cache_k.reshape(n_layers, n_phys, page * n_maps, d)
    cv = cache_v.reshape(n_layers, n_phys, page * n_heads, hw)
    n_parts = SAMPLE_PAGE_PARTS
    assert page % (n_parts * SUBLANES) == 0

    def kmap(which, part):
        return lambda b, t, pt, tab: (la, pt[b, 2 * jnp.minimum(t, n_steps - 1) + which], part, 0)

    def vmap(which, part):
        return lambda b, t, pt, tab: (la, pt[b, 2 * jnp.maximum(t - n_steps, 0) + which], part, 0)

    once = pl.Buffered(1)
    kblock = (1, 1, ck.shape[2] // n_parts, d)
    vblock = (1, 1, cv.shape[2] // n_parts, hw)
    kv_specs = ([pl.BlockSpec(kblock, kmap(w, p)) for w in range(2) for p in range(n_parts)]
                + [pl.BlockSpec(vblock, vmap(w, p)) for w in range(2) for p in range(n_parts)])
    per_seq = lambda shape, **kw: pl.BlockSpec((1,) + shape, lambda b, t, pt, tab: (b, 0, 0), **kw)
    const = lambda shape: pl.BlockSpec(shape, lambda b, t, pt, tab: (0, 0), pipeline_mode=once)
    qmap = lambda b, t, pt, tab: (tab[1, b * cycle + t] * nq + tab[3, b * cycle + t], tab[2, b * cycle + t])
    kvmap = lambda b, t, pt, tab: (tab[1, b * cycle + t], tab[2, b * cycle + t])
    qspec = pl.BlockSpec((tile, hw), qmap)
    kvspec = pl.BlockSpec((seq, hw), kvmap)
    smem = pl.BlockSpec(memory_space=pltpu.SMEM)
    grid_spec = pltpu.PrefetchScalarGridSpec(
        num_scalar_prefetch=2,
        grid=(bs, cycle),
        in_specs=[smem, smem,
                  per_seq((width, ncol), pipeline_mode=once), *kv_specs,
                  per_seq((n_new, width)), per_seq((n_new, width)), per_seq((n_new, width)),
                  const((page, ncol)), const((page, ncol)), const((1, ncol)), const((1, hw)),
                  const(perm.shape), const(expand.shape), const(mask.shape),
                  qspec, kvspec, kvspec, qspec,
                  pl.BlockSpec((1, 2, tile, tile), lambda b, t, pt, tab: (tab[2, b * cycle + t], 0, 0, 0))],
        out_specs=[per_seq((n_new, width)), qspec],
        scratch_shapes=[pltpu.VMEM((n_steps, 2 * page, ncol), F32), pltpu.VMEM((pad_new, ncol), F32),
                        pltpu.VMEM((1, ncol), F32), pltpu.VMEM((2, ncol // 2), F32),
                        pltpu.VMEM((n_heads * n_new, hw), F32), pltpu.VMEM((2 * page, width), BF16),
                        pltpu.VMEM((2, tile, 1), F32), pltpu.VMEM((2, tile, 1), F32),
                        pltpu.VMEM((2, tile, hw), F32)],
    )
    return pl.pallas_call(
        functools.partial(_fused_attn_kernel, n_parts=n_parts, n_steps=n_steps, n_new=n_new, tile=tile,
                          scale=ATTN_HEAD_DIM ** -0.5, lam_init=lam_init),
        grid_spec=grid_spec,
        out_shape=[jax.ShapeDtypeStruct((bs, n_new, width), F32), jax.ShapeDtypeStruct(q.shape, BF16)],
        compiler_params=pltpu.CompilerParams(dimension_semantics=("arbitrary", "arbitrary"),
                                             vmem_limit_bytes=V7X_FUSED_ATTN_VMEM_LIMIT_BYTES),
        name="attention",
    )(page_table, table, lam, rel_bias, qbd, *([ck] * (2 * n_parts)), *([cv] * (2 * n_parts)), k_new, v_new, g_new,
      bias_last, bias_new, far, subln_w.reshape(1, hw), perm, expand, mask, q, k, v, g, bias_tiles)


def _block_diag_queries(q, bs, n_new):
    d = ATTN_HEAD_DIM
    n_maps = q.shape[1] // d
    n_heads = n_maps // 2
    q5 = q.reshape(bs, n_new, n_heads, 2, d)
    qt = jnp.transpose(q5, (0, 2, 3, 4, 1))
    eye_h = jnp.eye(n_heads, dtype=q.dtype)
    eye_j = jnp.eye(2, dtype=q.dtype)
    out = (qt[:, :, :, :, None, None, :] * eye_j[None, None, :, None, :, None, None]
           * eye_h[None, :, None, None, None, :, None])
    return out.reshape(bs, n_maps * d, 2 * n_heads * n_new)


def _conv_kernel(x_ref, st_ref, w_ref, b_ref, y_ref, tail_ref, pad_ref, *, width):
    t = pl.program_id(2)
    tt = x_ref.shape[0]
    lo = 8 - (width - 1)

    @pl.when(t == 0)
    def _():
        pad_ref[lo:8, :] = st_ref[0]

    x = x_ref[...]
    pad_ref[8:8 + tt, :] = x
    acc = b_ref[...] + x * w_ref[width - 1:width, :]
    for j in range(width - 1):
        acc = acc + pad_ref[lo + j:lo + j + tt, :] * w_ref[j:j + 1, :]
    y_ref[...] = _silu(acc)
    tail = x[tt - (width - 1):, :]
    pad_ref[lo:8, :] = tail
    tail_ref[0] = tail


def _matmul_conv_kernel(a_ref, w_ref, cw_ref, cb_ref, y_ref, tail_ref, pad_ref, *, width, tiles_per_seq):
    i = pl.program_id(1)
    tm = a_ref.shape[0]
    lo = 8 - (width - 1)

    @pl.when(i % tiles_per_seq == 0)
    def _():
        pad_ref[lo:8, :] = jnp.zeros((width - 1, pad_ref.shape[1]), F32)

    chunk = min(2 * LANES, pad_ref.shape[1])
    for c0 in range(0, pad_ref.shape[1], chunk):
        cols = slice(c0, c0 + chunk)
        x = jnp.dot(a_ref[...], w_ref[:, cols], preferred_element_type=F32)
        pad_ref[8:8 + tm, cols] = x
        acc = cb_ref[:, cols] + x * cw_ref[width - 1:width, cols]
        for j in range(width - 1):
            acc = acc + pad_ref[lo + j:lo + j + tm, cols] * cw_ref[j:j + 1, cols]
        y_ref[:, cols] = _silu(acc)
        tail = x[tm - (width - 1):, :]
        pad_ref[lo:8, cols] = tail
        tail_ref[0, :, cols] = tail


def _matmul_conv(a, w, col0, conv_w, conv_b, batch, seq, tm, tn):
    m, k = a.shape
    width, c = conv_w.shape
    tm = min(tm, seq)
    assert seq % tm == 0 and c % tn == 0 and col0 % tn == 0 and tm >= width - 1
    cb0 = col0 // tn
    tiles_per_seq = seq // tm
    return pl.pallas_call(
        functools.partial(_matmul_conv_kernel, width=width, tiles_per_seq=tiles_per_seq),
        grid=(c // tn, m // tm),
        in_specs=[pl.BlockSpec((tm, k), lambda j, i: (i, 0)),
                  pl.BlockSpec((k, tn), lambda j, i: (0, cb0 + j)),
                  pl.BlockSpec((width, tn), lambda j, i: (0, j)),
                  pl.BlockSpec((1, tn), lambda j, i: (0, j))],
        out_specs=[pl.BlockSpec((tm, tn), lambda j, i: (i, j)),
                   pl.BlockSpec((1, width - 1, tn), lambda j, i: (i // tiles_per_seq, 0, j))],
        out_shape=[jax.ShapeDtypeStruct((m, c), F32),
                   jax.ShapeDtypeStruct((batch, width - 1, c), F32)],
        scratch_shapes=[pltpu.VMEM((8 + tm, tn), F32)],
        compiler_params=_cparams("parallel", "arbitrary"),
        name="matmul_conv",
    )(a, w, conv_w, conv_b.reshape(1, c))


def _conv(xbc, conv_state, conv_w, conv_b, batch, seq):
    m, c = xbc.shape
    width = conv_w.shape[0]
    tt = min(seq, 256)
    tc = 1024
    assert seq % tt == 0 and c % tc == 0 and tt >= width - 1
    nt = seq // tt
    return pl.pallas_call(
        functools.partial(_conv_kernel, width=width),
        grid=(batch, c // tc, nt),
        in_specs=[pl.BlockSpec((tt, tc), lambda b, j, t: (b * nt + t, j)),
                  pl.BlockSpec((1, width - 1, tc), lambda b, j, t: (b, 0, j)),
                  pl.BlockSpec((width, tc), lambda b, j, t: (0, j)),
                  pl.BlockSpec((1, tc), lambda b, j, t: (0, j))],
        out_specs=[pl.BlockSpec((tt, tc), lambda b, j, t: (b * nt + t, j)),
                   pl.BlockSpec((1, width - 1, tc), lambda b, j, t: (b, 0, j))],
        out_shape=[jax.ShapeDtypeStruct((m, c), F32),
                   jax.ShapeDtypeStruct((batch, width - 1, c), F32)],
        scratch_shapes=[pltpu.VMEM((8 + tt, tc), F32)],
        compiler_params=_cparams("parallel", "parallel", "arbitrary"),
        name="ssd_conv",
    )(xbc, conv_state, conv_w, conv_b.reshape(1, c))


def _split3(v):
    hi = v.astype(BF16)
    r1 = v - hi.astype(F32)
    mid = r1.astype(BF16)
    lo = (r1 - mid.astype(F32)).astype(BF16)
    return hi, mid, lo


def _dt_kernel(x_ref, bias_ref, alog_ref, dt_ref, cum_ref, dec_ref, *maybe_parts_ref):
    q = x_ref.shape[0]
    x = x_ref[...] + bias_ref[...]
    dt = jnp.maximum(x, 0.0) + jnp.log1p(jnp.exp(-jnp.abs(x)))
    c = dt * (-jnp.exp(alog_ref[...]))
    row = lax.broadcasted_iota(jnp.int32, c.shape, 0)
    shift = 1
    while shift < q:
        c = c + jnp.where(row >= shift, pltpu.roll(c, shift, axis=0), 0.0)
        shift *= 2
    dt_ref[...] = dt
    cum_ref[...] = c
    dec_ref[0] = jnp.exp(c[q - 1:q, :])
    if maybe_parts_ref:
        (parts_ref,) = maybe_parts_ref
        w_end = jnp.exp(c[q - 1:q, :] - c) * dt
        for i, v in enumerate((c, jnp.exp(c), w_end)):
            for j, part in enumerate(_split3(v)):
                parts_ref[3 * i + j] = part


def _dt_prep(dt_raw, dt_bias, a_log, q, with_parts):
    m, nh = dt_raw.shape
    row = pl.BlockSpec((q, nh), lambda i: (i, 0))
    vec = pl.BlockSpec((1, nh), lambda i: (0, 0))
    out_specs = [row, row, pl.BlockSpec((1, 1, nh), lambda i: (i, 0, 0))]
    out_shape = [jax.ShapeDtypeStruct((m, nh), F32), jax.ShapeDtypeStruct((m, nh), F32),
                 jax.ShapeDtypeStruct((m // q, 1, nh), F32)]
    if with_parts:
        out_specs.append(pl.BlockSpec((9, q, nh), lambda i: (0, i, 0)))
        out_shape.append(jax.ShapeDtypeStruct((9, m, nh), BF16))
    return pl.pallas_call(
        _dt_kernel,
        grid=(m // q,),
        in_specs=[row, vec, vec],
        out_specs=out_specs,
        out_shape=out_shape,
        compiler_params=_cparams("parallel"),
        name="ssd_dt_prep",
    )(dt_raw, dt_bias.reshape(1, nh), a_log.reshape(1, nh))


def _ssd_kernel(*refs, has_init, n_chunks):
    if has_init:
        (x_ref, b_ref, c_ref, z_ref, dt_ref, cum_ref, dtt_ref, cumt_ref, dec_ref, dskip_ref, gw_ref, s0_ref,
         y_ref, sout_ref, state_ref, ys_ref, xw_ref) = refs
    else:
        (x_ref, b_ref, c_ref, z_ref, dt_ref, cum_ref, dtt_ref, cumt_ref, dec_ref, dskip_ref, gw_ref,
         y_ref, sout_ref, state_ref, ys_ref, xw_ref) = refs
    ck = pl.program_id(2)
    q = x_ref.shape[0]
    r_heads = dt_ref.shape[3]
    p = SSM_HEAD_DIM

    @pl.when(ck == 0)
    def _():
        if has_init:
            state_ref[...] = s0_ref[0, 0].T
        else:
            state_ref[...] = jnp.zeros_like(state_ref)

    x = x_ref[...]
    xb = x.astype(BF16)
    bb = b_ref[...].astype(BF16)
    cb_ = c_ref[...].astype(BF16)
    cb = lax.dot_general(cb_, bb, (((1,), (1,)), ((), ())), preferred_element_type=F32)
    cs = jnp.dot(cb_, state_ref[...].astype(BF16), preferred_element_type=F32)
    dt = dt_ref[0, 0]
    cum = cum_ref[0, 0]
    dtt = dtt_ref[0, 0]
    cumt = cumt_ref[0, 0]
    w_end = jnp.exp(cum[q - 1:q, :] - cum) * dt
    ecum = jnp.exp(cum)
    li = lax.broadcasted_iota(jnp.int32, (q, q), 0)
    si = lax.broadcasted_iota(jnp.int32, (q, q), 1)
    causal = li >= si
    for r in range(r_heads):
        cols = slice(r * p, (r + 1) * p)
        seg = cum[:, r:r + 1] - cumt[r:r + 1, :]
        decay = jnp.exp(jnp.where(causal, seg, NEG_INF))
        mh = (cb * decay * dtt[r:r + 1, :]).astype(BF16)
        ys_ref[:, cols] = (jnp.dot(mh, xb[:, cols], preferred_element_type=F32)
                           + cs[:, cols] * ecum[:, r:r + 1])
        xw_ref[:, cols] = (x[:, cols] * w_end[:, r:r + 1]).astype(BF16)
    ds = lax.dot_general(bb, xw_ref[...], (((0,), (0,)), ((), ())), preferred_element_type=F32)
    state_ref[...] = state_ref[...] * dec_ref[0, 0, 0] + ds

    y = ys_ref[...] + x * dskip_ref[...]
    y = y * _silu(z_ref[...])
    y = y * lax.rsqrt(jnp.mean(y * y, axis=-1, keepdims=True) + NORM_EPS) * gw_ref[...]
    y_ref[...] = y.astype(y_ref.dtype)

    @pl.when(ck == n_chunks - 1)
    def _():
        sout_ref[0, 0] = state_ref[...].T


def _ssd_scan(act, z, dt, cum, dec_last, d_skip, gnorm_w, state0, batch, seq, q, out_dtype):
    m = act.shape[0]
    inner = z.shape[1]
    g = SSM_GROUPS
    n = SSM_STATE
    gw_cols = inner // g
    r = gw_cols // SSM_HEAD_DIM
    nc = seq // q
    assert n == LANES and gw_cols % LANES == 0
    dt4 = jnp.transpose(dt.reshape(batch, seq, g, r), (0, 2, 1, 3))
    cum4 = jnp.transpose(cum.reshape(batch, seq, g, r), (0, 2, 1, 3))
    dtt4 = jnp.transpose(dt4, (0, 1, 3, 2))
    cumt4 = jnp.transpose(cum4, (0, 1, 3, 2))
    dec5 = jnp.repeat(dec_last.reshape(batch, nc, g, 1, r), SSM_HEAD_DIM, axis=-1)
    dskip_row = jnp.repeat(d_skip, SSM_HEAD_DIM).reshape(1, inner)

    xcol0 = inner // gw_cols
    row_blk = lambda off: pl.BlockSpec((q, gw_cols), lambda b, gi, c: (b * nc + c, off + gi))
    bspec = pl.BlockSpec((q, n), lambda b, gi, c: (b * nc + c, inner // n + gi))
    cspec = pl.BlockSpec((q, n), lambda b, gi, c: (b * nc + c, inner // n + g + gi))
    tcol = pl.BlockSpec((1, 1, q, r), lambda b, gi, c: (b, gi, c, 0))
    trow = pl.BlockSpec((1, 1, r, q), lambda b, gi, c: (b, gi, 0, c))
    vec = pl.BlockSpec((1, gw_cols), lambda b, gi, c: (0, gi))
    sspec = pl.BlockSpec((1, 1, gw_cols, n), lambda b, gi, c: (b, gi, 0, 0))
    in_specs = [row_blk(0), bspec, cspec, row_blk(0), tcol, tcol, trow, trow,
                pl.BlockSpec((1, 1, 1, 1, gw_cols), lambda b, gi, c: (b, c, gi, 0, 0)), vec, vec]
    args = [act, act, act, z, dt4, cum4, dtt4, cumt4, dec5, dskip_row, gnorm_w.reshape(1, inner)]
    if state0 is not None:
        in_specs.append(sspec)
        args.append(state0.reshape(batch, g, gw_cols, n))
    del xcol0
    y, s_out = pl.pallas_call(
        functools.partial(_ssd_kernel, has_init=state0 is not None, n_chunks=nc),
        grid=(batch, g, nc),
        in_specs=in_specs,
        out_specs=[row_blk(0), sspec],
        out_shape=[jax.ShapeDtypeStruct((m, inner), out_dtype),
                   jax.ShapeDtypeStruct((batch, g, gw_cols, n), F32)],
        scratch_shapes=[pltpu.VMEM((n, gw_cols), F32), pltpu.VMEM((q, gw_cols), F32),
                        pltpu.VMEM((q, gw_cols), BF16)],
        compiler_params=_cparams("parallel", "parallel", "arbitrary"),
        name="ssd_scan_stateful",
    )(*args)
    return y, s_out


def _expansion_matrix(r_heads):
    p = SSM_HEAD_DIM
    e = np.zeros((9 * r_heads, r_heads * (LANES + 2 * p)), np.float32)
    for k in range(9):
        for r in range(r_heads):
            if k < 3:
                lo, w = r * LANES, LANES
            else:
                lo, w = r_heads * LANES + (k // 3 - 1) * r_heads * p + r * p, p
            e[k * r_heads + r, lo:lo + w] = 1.0
    return jnp.asarray(e, BF16)


def _ssd_chunk_kernel(x_ref, b_ref, c_ref, z_ref, parts_ref, dtt_ref, cumt_ref, e_ref, dec_ref, dskip_ref, gw_ref,
                      y_ref, sout_ref, state_ref, xl_ref, ys_ref, *, n_chunks):
    ck = pl.program_id(2)
    q = x_ref.shape[0]
    r_heads = dtt_ref.shape[2]
    p = SSM_HEAD_DIM

    @pl.when(ck == 0)
    def _():
        state_ref[...] = jnp.zeros_like(state_ref)

    x = x_ref[...]
    bb = b_ref[...].astype(BF16)
    cb_ = c_ref[...].astype(BF16)
    cb = lax.dot_general(cb_, bb, (((1,), (1,)), ((), ())), preferred_element_type=F32)
    cs = jnp.dot(cb_, state_ref[...].astype(BF16), preferred_element_type=F32)
    xl_ref[...] = jnp.dot(parts_ref[0, 0], e_ref[...], preferred_element_type=F32)
    dtt = dtt_ref[0, 0]
    cumt = cumt_ref[0, 0]
    li = lax.broadcasted_iota(jnp.int32, (q, q), 0)
    si = lax.broadcasted_iota(jnp.int32, (q, q), 1)
    causal = li >= si
    first = lax.broadcasted_iota(jnp.int32, (q, 2 * p), 1) < p
    for pair in range(r_heads // 2):
        mhs = []
        for r in (2 * pair, 2 * pair + 1):
            seg = xl_ref[:, r * LANES:(r + 1) * LANES] - cumt[r:r + 1, :]
            decay = jnp.exp(jnp.where(causal, seg, NEG_INF))
            mhs.append((cb * decay * dtt[r:r + 1, :]).astype(BF16))
        xp = x[:, pair * 2 * p:(pair + 1) * 2 * p]
        rhs = jnp.concatenate([jnp.where(first, xp, 0.0).astype(BF16),
                               jnp.where(first, 0.0, xp).astype(BF16)], axis=0)
        ys_ref[:, pair * 2 * p:(pair + 1) * 2 * p] = jnp.dot(jnp.concatenate(mhs, axis=1), rhs,
                                                             preferred_element_type=F32)
    c0 = r_heads * LANES
    ecum = xl_ref[:, c0:c0 + r_heads * p]
    w_end = xl_ref[:, c0 + r_heads * p:c0 + 2 * r_heads * p]
    xw = (x * w_end).astype(BF16)
    ds = lax.dot_general(bb, xw, (((0,), (0,)), ((), ())), preferred_element_type=F32)
    state_ref[...] = state_ref[...] * dec_ref[0, 0, 0] + ds

    y = ys_ref[...] + cs * ecum + x * dskip_ref[...]
    y = y * _silu(z_ref[...])
    y = y * lax.rsqrt(jnp.mean(y * y, axis=-1, keepdims=True) + NORM_EPS) * gw_ref[...]
    y_ref[...] = y.astype(y_ref.dtype)

    @pl.when(ck == n_chunks - 1)
    def _():
        sout_ref[0, 0] = state_ref[...].T


def _ssd_scan_chunked(act, z, dt, cum, dec_last, parts, d_skip, gnorm_w, batch, seq, q, out_dtype):
    m = act.shape[0]
    inner = z.shape[1]
    g = SSM_GROUPS
    n = SSM_STATE
    gw_cols = inner // g
    r = gw_cols // SSM_HEAD_DIM
    nc = seq // q
    assert n == LANES and q == LANES and gw_cols % LANES == 0 and r % 2 == 0
    dtt4 = jnp.transpose(dt.reshape(batch, seq, g, r), (0, 2, 3, 1))
    cumt4 = jnp.transpose(cum.reshape(batch, seq, g, r), (0, 2, 3, 1))
    parts4 = jnp.transpose(parts.reshape(9, batch, seq, g, r), (1, 3, 2, 0, 4)).reshape(batch, g, seq, 9 * r)
    dec5 = jnp.repeat(dec_last.reshape(batch, nc, g, 1, r), SSM_HEAD_DIM, axis=-1)
    dskip_row = jnp.repeat(d_skip, SSM_HEAD_DIM).reshape(1, inner)
    emat = _expansion_matrix(r)

    row_blk = pl.BlockSpec((q, gw_cols), lambda b, gi, c: (b * nc + c, gi))
    bspec = pl.BlockSpec((q, n), lambda b, gi, c: (b * nc + c, inner // n + gi))
    cspec = pl.BlockSpec((q, n), lambda b, gi, c: (b * nc + c, inner // n + g + gi))
    trow = pl.BlockSpec((1, 1, r, q), lambda b, gi, c: (b, gi, 0, c))
    vec = pl.BlockSpec((1, gw_cols), lambda b, gi, c: (0, gi))
    sspec = pl.BlockSpec((1, 1, gw_cols, n), lambda b, gi, c: (b, gi, 0, 0))
    y, s_out = pl.pallas_call(
        functools.partial(_ssd_chunk_kernel, n_chunks=nc),
        grid=(batch, g, nc),
        in_specs=[row_blk, bspec, cspec, row_blk,
                  pl.BlockSpec((1, 1, q, 9 * r), lambda b, gi, c: (b, gi, c, 0)), trow, trow,
                  pl.BlockSpec(emat.shape, lambda b, gi, c: (0, 0)),
                  pl.BlockSpec((1, 1, 1, 1, gw_cols), lambda b, gi, c: (b, c, gi, 0, 0)), vec, vec],
        out_specs=[row_blk, sspec],
        out_shape=[jax.ShapeDtypeStruct((m, inner), out_dtype),
                   jax.ShapeDtypeStruct((batch, g, gw_cols, n), F32)],
        scratch_shapes=[pltpu.VMEM((n, gw_cols), F32), pltpu.VMEM((q, emat.shape[1]), F32),
                        pltpu.VMEM((q, gw_cols), F32)],
        compiler_params=_cparams("parallel", "parallel", "arbitrary"),
        name="ssd_scan_chunked",
    )(act, act, act, z, parts4, dtt4, cumt4, emat, dec5, dskip_row, gnorm_w.reshape(1, inner))
    return y, s_out


def _attn_layer(hp, hs, w_in, w_out, lambda_qk_l, subln_w_l, rel_bias, cache_k, cache_v, la, page_table,
                batch, seq, bs, n_new, lam_init):
    width = w_out.shape[0]
    tile = min(seq, 512)
    assert seq % tile == 0 and tile + 1 >= FAR_DISTANCE
    page = cache_k.shape[2]

    bias_tiles = _bias_tiles(rel_bias, tile)
    bias_last, bias_new, far, lam = _sample_prep(rel_bias, lambda_qk_l, page, n_new, lam_init)

    (q,) = _matmul(hp, w_in, 0, width, [BF16], 512, 1024)
    k, kb = _matmul(hp, w_in, width, width, [F32, BF16], 512, 1024)
    v, vb = _matmul(hp, w_in, 2 * width, width, [F32, BF16], 512, 1024)
    (g,) = _matmul(hp, w_in, 3 * width, width, [F32], 512, 1024)

    (q2,) = _matmul(hs, w_in, 0, width, [BF16], 512, 1024)
    (k2,) = _matmul(hs, w_in, width, width, [F32], 512, 1024)
    (v2,) = _matmul(hs, w_in, 2 * width, width, [F32], 512, 1024)
    (g2,) = _matmul(hs, w_in, 3 * width, width, [F32], 512, 1024)
    qbd = _block_diag_queries(q2, bs, n_new)
    og2, og = _attention(page_table, lam, rel_bias, qbd, cache_k, cache_v, la,
                         k2.reshape(bs, n_new, width), v2.reshape(bs, n_new, width), g2.reshape(bs, n_new, width),
                         bias_last, bias_new, far, subln_w_l, q, kb, vb, g, bias_tiles, batch, seq, tile, n_new,
                         lam_init)
    (op,) = _matmul(og, w_out, 0, w_out.shape[1], [F32], 512, 1024)
    (os_,) = _matmul(og2.reshape(bs * n_new, width).astype(BF16), w_out, 0, w_out.shape[1], [F32], 512, 1024)
    return op, os_, k, v, k2, v2


def _ssd_layer(h, conv_state, ssm_state, w_in, conv_w, conv_b, dt_bias, a_log, d_skip, gnorm_w, w_out,
               batch, seq):
    inner = w_out.shape[0]
    conv_dim = conv_w.shape[1]
    n_heads = dt_bias.shape[0]
    q = SSD_CHUNK if seq % SSD_CHUNK == 0 else seq
    (z,) = _matmul(h, w_in, 0, inner, [F32], 512, 1024)
    (dt_raw,) = _matmul(h, w_in, inner + conv_dim, n_heads, [F32], 512, n_heads)
    if conv_state is None:
        act, conv_out = _matmul_conv(h, w_in, inner, conv_w, conv_b, batch, seq, 512, 1024)
    else:
        (xbc,) = _matmul(h, w_in, inner, conv_dim, [F32], 512, 1024)
        act, conv_out = _conv(xbc, conv_state, conv_w, conv_b, batch, seq)
    if ssm_state is None and q == LANES:
        dt, cum, dec_last, parts = _dt_prep(dt_raw, dt_bias, a_log, q, True)
        y, s_out = _ssd_scan_chunked(act, z, dt, cum, dec_last, parts, d_skip, gnorm_w, batch, seq, q, BF16)
    else:
        dt, cum, dec_last = _dt_prep(dt_raw, dt_bias, a_log, q, False)
        y, s_out = _ssd_scan(act, z, dt, cum, dec_last, d_skip, gnorm_w, ssm_state, batch, seq, q, F32)
    (o,) = _matmul(y.astype(BF16), w_out, 0, w_out.shape[1], [F32], 512, 512)
    return o, conv_out, s_out.reshape(batch, n_heads, SSM_HEAD_DIM, SSM_STATE)


def kernel(x_prompt, x_sample, cache_k, cache_v, page_table, state_conv, state_ssm, norm_pre, norm_post, rel_bias,
           w_attn_in, lambda_qk, subln_w, w_attn_out, w_ssm_in, conv_w, conv_b, dt_bias, a_log, d_skip, gnorm_w,
           w_ssm_out):
    bp, sp, dm = x_prompt.shape
    bs, ss, _ = x_sample.shape
    depth = norm_pre.shape[0]
    xp = x_prompt.reshape(bp * sp, dm)
    xs = x_sample.reshape(bs * ss, dm)
    kp_l, vp_l, ks_l, vs_l, cp_l, sp_l, cs_l, ss_l = [], [], [], [], [], [], [], []
    for i in range(depth):
        hp = _rmsnorm(xp, norm_pre[i], NORM_EPS, BF16)
        hs = _rmsnorm(xs, norm_pre[i], NORM_EPS, BF16)
        if i % N_MIXERS == 0:
            la = i // N_MIXERS
            lam_init = 0.8 - 0.6 * math.exp(-0.3 * i)
            op, os_, k, v, k2, v2 = _attn_layer(
                hp, hs, w_attn_in[la].astype(BF16), w_attn_out[la].astype(BF16), lambda_qk[la], subln_w[la],
                rel_bias, cache_k, cache_v, la, page_table, bp, sp, bs, ss, lam_init)
            n_maps = k.shape[1] // ATTN_HEAD_DIM
            kp_l.append(k.reshape(bp, sp, n_maps, ATTN_HEAD_DIM))
            vp_l.append(v.reshape(bp, sp, n_maps // 2, 2 * ATTN_HEAD_DIM))
            ks_l.append(k2.reshape(bs, ss, n_maps, ATTN_HEAD_DIM))
            vs_l.append(v2.reshape(bs, ss, n_maps // 2, 2 * ATTN_HEAD_DIM))
        else:
            ls = i // N_MIXERS
            wts = (w_ssm_in[ls].astype(BF16), conv_w[ls], conv_b[ls], dt_bias[ls], a_log[ls], d_skip[ls],
                   gnorm_w[ls], w_ssm_out[ls].astype(BF16))
            op, cpn, spn = _ssd_layer(hp, None, None, *wts, bp, sp)
            os_, csn, ssn = _ssd_layer(hs, state_conv[ls], state_ssm[ls], *wts, bs, ss)
            cp_l.append(cpn)
            sp_l.append(spn)
            cs_l.append(csn)
            ss_l.append(ssn)
        xp = _residual_norm(xp, op, norm_post[i], NORM_EPS)
        xs = _residual_norm(xs, os_, norm_post[i], NORM_EPS)
    return (xp.reshape(bp, sp, dm), xs.reshape(bs, ss, dm),
            jnp.stack(kp_l), jnp.stack(vp_l), jnp.stack(ks_l), jnp.stack(vs_l),
            jnp.stack(cp_l), jnp.stack(sp_l), jnp.stack(cs_l), jnp.stack(ss_l))
```

```python
import functools
import math

import numpy as np
import jax
import jax.numpy as jnp
from jax import lax
from jax.experimental import pallas as pl
from jax.experimental.pallas import tpu as pltpu

ATTN_HEAD_DIM = 128
N_BUCKETS = 32
MAX_DISTANCE = 128
SUBLN_EPS = 1e-5
NORM_EPS = 1e-6
SSM_HEAD_DIM = 64
SSM_GROUPS = 8
SSM_STATE = 128
CONV_WIDTH = 4
SSD_CHUNK = 128
N_MIXERS = 2

V7X_VMEM_LIMIT_BYTES = 56 * 1024 * 1024
V7X_FUSED_ATTN_VMEM_LIMIT_BYTES = 60 * 1024 * 1024
LANES = 128

F32 = jnp.float32
BF16 = jnp.bfloat16
NEG_INF = float("-inf")


def _cparams(*sem):
    return pltpu.CompilerParams(dimension_semantics=sem, vmem_limit_bytes=V7X_VMEM_LIMIT_BYTES)


def _bucket_thresholds():
    n = np.arange(0, 4 * MAX_DISTANCE)
    max_exact = N_BUCKETS // 2
    nf = np.maximum(n, max_exact).astype(np.float32)
    large = max_exact + (np.log(nf / max_exact) / math.log(MAX_DISTANCE / max_exact)
                         * (N_BUCKETS - max_exact)).astype(np.int32)
    bucket = np.where(n < max_exact, n, np.minimum(large, N_BUCKETS - 1))
    return [int(np.argmax(bucket >= b)) for b in range(N_BUCKETS)]


BUCKET_THR = _bucket_thresholds()
FAR_DISTANCE = BUCKET_THR[N_BUCKETS - 1]


def _bias_of_distance(n, rel_of_bucket):
    v = jnp.broadcast_to(rel_of_bucket(0), n.shape).astype(F32)
    for b in range(1, N_BUCKETS):
        v = jnp.where(n >= BUCKET_THR[b], rel_of_bucket(b), v)
    return v


def _bias_tiles_kernel(rel_ref, out_ref, *, tile):
    h = pl.program_id(0)
    i = lax.broadcasted_iota(jnp.int32, (tile, tile), 0)
    j = lax.broadcasted_iota(jnp.int32, (tile, tile), 1)
    d = i - j
    rel = lambda b: rel_ref[b, h]
    out_ref[0, 0] = jnp.where(d >= 0, _bias_of_distance(d, rel), NEG_INF)
    out_ref[0, 1] = _bias_of_distance(d + tile, rel)


def _bias_tiles(rel_bias, tile):
    n_heads = rel_bias.shape[1]
    return pl.pallas_call(
        functools.partial(_bias_tiles_kernel, tile=tile),
        grid=(n_heads,),
        in_specs=[pl.BlockSpec(memory_space=pltpu.SMEM)],
        out_specs=pl.BlockSpec((1, 2, tile, tile), lambda h: (h, 0, 0, 0)),
        out_shape=jax.ShapeDtypeStruct((n_heads, 2, tile, tile), F32),
        compiler_params=_cparams("arbitrary"),
        name="bias_tiles",
    )(rel_bias)


def _sample_prep_kernel(relc_ref, lq_ref, bias_last_ref, bias_new_ref, far_ref, lam_ref, *, page, n_new, lam_init):
    ncol = relc_ref.shape[1]
    key = lax.broadcasted_iota(jnp.int32, (page, ncol), 0)
    col = lax.broadcasted_iota(jnp.int32, (page, ncol), 1)
    qi = col % n_new
    rel = lambda b: relc_ref[b:b + 1, :]
    bias_last_ref[...] = _bias_of_distance(page + qi - key, rel)
    d = qi - key
    bias_new_ref[...] = jnp.where(d >= 0, _bias_of_distance(d, rel), NEG_INF)
    far_ref[...] = relc_ref[N_BUCKETS - 1:N_BUCKETS, :]
    lq = lq_ref[...]
    lam = (jnp.exp(jnp.sum(lq[0:1] * lq[1:2], axis=-1, keepdims=True))
           - jnp.exp(jnp.sum(lq[2:3] * lq[3:4], axis=-1, keepdims=True)) + lam_init)
    lam_ref[...] = lam


def _sample_prep(rel_bias, lambda_qk_l, page, n_new, lam_init):
    n_heads = rel_bias.shape[1]
    ncol = 2 * n_heads * n_new
    relc = jnp.tile(jnp.repeat(rel_bias, n_new, axis=1), (1, 2))
    full = lambda shape: pl.BlockSpec(shape, lambda: (0,) * len(shape))
    return pl.pallas_call(
        functools.partial(_sample_prep_kernel, page=page, n_new=n_new, lam_init=lam_init),
        in_specs=[full(relc.shape), full(lambda_qk_l.shape)],
        out_specs=[full((page, ncol)), full((page, ncol)), full((1, ncol)), full((1, 1))],
        out_shape=[jax.ShapeDtypeStruct((page, ncol), F32), jax.ShapeDtypeStruct((page, ncol), F32),
                   jax.ShapeDtypeStruct((1, ncol), F32), jax.ShapeDtypeStruct((1, 1), F32)],
    )(relc, lambda_qk_l)


def _rmsnorm_kernel(x_ref, w_ref, o_ref, *, eps):
    x = x_ref[...]
    y = x * lax.rsqrt(jnp.mean(x * x, axis=-1, keepdims=True) + eps)
    o_ref[...] = (y * w_ref[...]).astype(o_ref.dtype)


def _rmsnorm(x2d, w, eps, out_dtype):
    m, d = x2d.shape
    tm = min(m, 256)
    return pl.pallas_call(
        functools.partial(_rmsnorm_kernel, eps=eps),
        grid=(m // tm,),
        in_specs=[pl.BlockSpec((tm, d), lambda i: (i, 0)), pl.BlockSpec((1, d), lambda i: (0, 0))],
        out_specs=pl.BlockSpec((tm, d), lambda i: (i, 0)),
        out_shape=jax.ShapeDtypeStruct((m, d), out_dtype),
        compiler_params=_cparams("parallel"),
        name="rmsnorm",
    )(x2d, w.reshape(1, d))


def _residual_norm_kernel(x_ref, y_ref, w_ref, *refs, eps):
    y = y_ref[...]
    yn = y * lax.rsqrt(jnp.mean(y * y, axis=-1, keepdims=True) + eps)
    x = x_ref[...] + yn * w_ref[...]
    if len(refs) == 1:
        (o_ref,) = refs
    else:
        w_next_ref, o_ref, h_ref = refs
        hn = x * lax.rsqrt(jnp.mean(x * x, axis=-1, keepdims=True) + eps)
        h_ref[...] = (hn * w_next_ref[...]).astype(h_ref.dtype)
    o_ref[...] = x


def _residual_norm(x2d, y2d, w, eps, w_next=None):
    m, d = x2d.shape
    tm = min(m, 256)
    row = pl.BlockSpec((tm, d), lambda i: (i, 0))
    vec = pl.BlockSpec((1, d), lambda i: (0, 0))
    fused = w_next is not None
    outs = pl.pallas_call(
        functools.partial(_residual_norm_kernel, eps=eps),
        grid=(m // tm,),
        in_specs=[row, row, vec] + ([vec] if fused else []),
        out_specs=[row, row] if fused else [row],
        out_shape=[jax.ShapeDtypeStruct((m, d), F32)] + ([jax.ShapeDtypeStruct((m, d), BF16)] if fused else []),
        compiler_params=_cparams("parallel"),
        name="residual_norm",
    )(x2d, y2d, w.reshape(1, d), *([w_next.reshape(1, d)] if fused else []))
    return outs if fused else (outs[0], None)


def _matmul_kernel(a_ref, w_ref, *o_refs):
    acc = jnp.dot(a_ref[...], w_ref[...], preferred_element_type=F32)
    for o_ref in o_refs:
        o_ref[...] = acc.astype(o_ref.dtype)


def _matmul(a, w, col0, n, out_dtypes, tm, tn):
    m, k = a.shape
    tm = min(tm, m)
    tn = min(tn, n)
    assert m % tm == 0 and n % tn == 0 and col0 % tn == 0
    cb0 = col0 // tn
    outs = pl.pallas_call(
        _matmul_kernel,
        grid=(n // tn, m // tm),
        in_specs=[pl.BlockSpec((tm, k), lambda j, i: (i, 0)),
                  pl.BlockSpec((k, tn), lambda j, i: (0, cb0 + j))],
        out_specs=[pl.BlockSpec((tm, tn), lambda j, i: (i, j)) for _ in out_dtypes],
        out_shape=[jax.ShapeDtypeStruct((m, n), dt) for dt in out_dtypes],
        compiler_params=_cparams("parallel", "parallel"),
        name="matmul",
    )(a, w)
    return outs


def _silu(g):
    return g * (1.0 / (1.0 + jnp.exp(-g)))


def _subln_gate(o, g, sw, lam_init):
    o = o * lax.rsqrt(jnp.mean(o * o, axis=-1, keepdims=True) + SUBLN_EPS) * sw
    return o * (1.0 - lam_init) * _silu(g)


def _prompt_attn_unit(unit, lam_ref, rel_ref, q_ref, k_ref, v_ref, g_ref, bias_ref, sw_ref, o_ref,
                      m_ref, l_ref, acc_ref, *, tile, scale, lam_init):
    h, kb, kind, last = unit
    d = ATTN_HEAD_DIM
    far = rel_ref[N_BUCKETS - 1, h]
    row0 = pl.multiple_of(kb * tile, tile)

    def scores(j, kblk, bias):
        s = lax.dot_general(q_ref[:, j * d:(j + 1) * d], kblk[:, j * d:(j + 1) * d],
                            (((1,), (1,)), ((), ())), preferred_element_type=F32)
        return s * scale + bias

    def row_reduce(x, fold, reduce):
        acc = x[:, 0:LANES]
        for i in range(1, x.shape[1] // LANES):
            acc = fold(acc, x[:, i * LANES:(i + 1) * LANES])
        return reduce(acc, axis=-1, keepdims=True)

    def block(bias, first):
        kblk = k_ref[pl.ds(row0, tile), :]
        vblk = v_ref[pl.ds(row0, tile), :]
        for j in range(2):
            s = scores(j, kblk, bias)
            row_max = row_reduce(s, jnp.maximum, jnp.max)
            m = row_max if first else jnp.maximum(m_ref[j], row_max)
            p = jnp.exp(s - m)
            row_sum = row_reduce(p, jnp.add, jnp.sum)
            pv = jnp.dot(p.astype(BF16), vblk, preferred_element_type=F32)
            if first:
                l_ref[j] = row_sum
                acc_ref[j] = pv
            else:
                alpha = jnp.exp(m_ref[j] - m)
                l_ref[j] = alpha * l_ref[j] + row_sum
                acc_ref[j] = alpha * acc_ref[j] + pv
            m_ref[j] = m

    @pl.when(kind == 0)
    def _():
        block(bias_ref[0, 0], True)

    @pl.when(kind == 1)
    def _():
        block(bias_ref[0, 1], False)

    @pl.when(kind == 2)
    def _():
        block(far, False)

    @pl.when(last == 1)
    def _():
        lam = lam_ref[0, 0]
        o = acc_ref[0] / l_ref[0] - lam * (acc_ref[1] / l_ref[1])
        o_ref[...] = _subln_gate(o, g_ref[...], sw_ref[...], lam_init).astype(o_ref.dtype)


def _prompt_attn_schedule(batch, n_heads, nq, n_cycles, cycle_steps):
    units = []
    for b in range(batch):
        for h in range(n_heads):
            for qb in range(nq):
                kbs = [(qb, 0)] + ([(qb - 1, 1)] if qb >= 1 else []) + [(kb, 2) for kb in range(qb - 1)]
                for i, (kb, kind) in enumerate(kbs):
                    units.append((1, b, h, qb, kb, kind, int(i == len(kbs) - 1)))
    per_cycle = -(-len(units) // n_cycles)
    assert per_cycle <= cycle_steps
    tab = np.zeros((7, n_cycles * cycle_steps), np.int32)
    current = units[0]
    for c in range(n_cycles):
        mine = units[c * per_cycle:(c + 1) * per_cycle]
        steps = sorted(range(cycle_steps - 1, -1, -1)[:len(mine)])
        by_step = dict(zip(steps, mine))
        for t in range(cycle_steps):
            if t in by_step:
                current = by_step[t]
                tab[:, c * cycle_steps + t] = current
            else:
                tab[:, c * cycle_steps + t] = (0,) + current[1:4] + (0, 0, 0)
    return tab


SUBLANES = 8
SAMPLE_PAGE_PARTS = 1


def _sample_attn_constants(page, n_maps, n_heads, n_new):
    rb = SUBLANES * n_maps
    perm = np.zeros((rb, rb), np.float32)
    for c in range(n_maps):
        for k8 in range(SUBLANES):
            perm[c * SUBLANES + k8, k8 * n_maps + c] = 1.0
    expand = np.zeros((page, page * n_heads), np.float32)
    for key in range(page):
        expand[key, key * n_heads:(key + 1) * n_heads] = 1.0
    row_head = np.arange(n_heads * n_new)[:, None] // n_new
    col_head = np.arange(page * n_heads)[None, :] % n_heads
    mask = (row_head == col_head).astype(np.float32)
    return jnp.asarray(perm, BF16), jnp.asarray(expand, BF16), jnp.asarray(mask, F32)


def _sample_attn_kernel(pt_ref, lam_ref, qbd_ref, *refs, n_parts, n_steps, n_new, scale, lam_init):
    ka_refs, kb_refs = refs[0:n_parts], refs[n_parts:2 * n_parts]
    va_refs, vb_refs = refs[2 * n_parts:3 * n_parts], refs[3 * n_parts:4 * n_parts]
    (kn_ref, vn_ref, g_ref, bias_last_ref, bias_new_ref, far_ref, sw_ref, perm_ref, expand_ref, mask_ref, o_ref,
     s_ref, sn_ref, m_ref, w_ref, acc_ref, kscr_ref) = refs[4 * n_parts:]
    t = pl.program_id(1)
    d = ATTN_HEAD_DIM
    rb = perm_ref.shape[0]
    n_maps = rb // SUBLANES
    page = n_parts * ka_refs[0].shape[2] // n_maps

    def page_rows(part_refs):
        return jnp.concatenate([r[0, 0] for r in part_refs], axis=0)

    n_blocks = page // SUBLANES
    half = s_ref.shape[2] // 2
    hw = 2 * d
    n_heads = n_maps // 2
    pad_new = sn_ref.shape[0]

    def gather_keys(k_refs, row0):
        kb16 = page_rows(k_refs).astype(BF16)
        w = jnp.concatenate([kb16[b * rb:(b + 1) * rb] for b in range(n_blocks)], axis=1)
        out = jnp.dot(perm_ref[...], w, preferred_element_type=F32)

        def block_rows(b):
            return jnp.concatenate([out[c * SUBLANES:(c + 1) * SUBLANES, b * d:(b + 1) * d] for c in range(n_maps)],
                                   axis=1)

        for b in range(0, n_blocks, 2):
            r0 = row0 + b * SUBLANES
            kscr_ref[r0:r0 + 2 * SUBLANES, :] = jnp.concatenate([block_rows(b), block_rows(b + 1)],
                                                                 axis=0).astype(BF16)

    @pl.when(t < n_steps)
    def _():
        gather_keys(ka_refs, 0)
        gather_keys(kb_refs, page)
        s = jnp.dot(kscr_ref[...], qbd_ref[0], preferred_element_type=F32) * scale
        s_ref[t, 0:page, :] = s[0:page] + far_ref[...]
        s_ref[t, page:2 * page, :] = s[page:] + jnp.where(t == n_steps - 1, bias_last_ref[...], far_ref[...])
        pmax = jnp.max(s_ref[t], axis=0, keepdims=True)

        @pl.when(t == 0)
        def _():
            m_ref[...] = pmax

        @pl.when(t > 0)
        def _():
            m_ref[...] = jnp.maximum(m_ref[...], pmax)

    @pl.when(t == n_steps - 1)
    def _():
        kn = jnp.concatenate([kn_ref[0], jnp.zeros((pad_new - n_new, kn_ref.shape[2]), F32)], axis=0)
        sn = (jnp.dot(kn.astype(BF16), qbd_ref[0], preferred_element_type=F32) * scale
              + bias_new_ref[0:pad_new, :])
        m = jnp.maximum(m_ref[...], jnp.max(sn, axis=0, keepdims=True))
        pn = jnp.exp(sn - m)
        sn_ref[...] = pn

        def body(slot, l):
            p = jnp.exp(s_ref[slot] - m)
            s_ref[slot] = p
            return l + jnp.sum(p, axis=0, keepdims=True)

        l = lax.fori_loop(0, n_steps, body, jnp.sum(pn, axis=0, keepdims=True))
        inv = 1.0 / l
        w_ref[0:1, :] = inv[:, :half]
        w_ref[1:2, :] = lam_ref[0, 0] * inv[:, half:]
        acc_ref[...] = jnp.zeros_like(acc_ref)

    def combined_probs_t(p):
        a = p[:, :half] * w_ref[0:1, :] - p[:, half:] * w_ref[1:2, :]
        return a.T.astype(BF16)

    def accumulate_page(p, v_refs):
        at = combined_probs_t(p)
        spread = jnp.dot(at, expand_ref[...], preferred_element_type=F32) * mask_ref[...]
        acc_ref[...] += jnp.dot(spread.astype(BF16), page_rows(v_refs).astype(BF16), preferred_element_type=F32)

    @pl.when(t >= n_steps)
    def _():
        p = s_ref[t - n_steps]
        accumulate_page(p[0:page], va_refs)
        accumulate_page(p[page:], vb_refs)

    @pl.when(t == 2 * n_steps - 1)
    def _():
        pn = jnp.concatenate([sn_ref[...], jnp.zeros((page - pad_new, 2 * half), F32)], axis=0)
        at = combined_probs_t(pn)
        rows = 2 * n_new
        for h in range(n_heads):
            vh = jnp.concatenate([vn_ref[0, :, h * hw:(h + 1) * hw], jnp.zeros((page - n_new, hw), F32)], axis=0)
            r0 = (h * n_new) // rows * rows
            res = jnp.dot(at[r0:r0 + rows, :], vh.astype(BF16), preferred_element_type=F32)
            off = h * n_new - r0
            oh = acc_ref[h * n_new:(h + 1) * n_new, :] + res[off:off + n_new, :]
            cols = slice(h * hw, (h + 1) * hw)
            o_ref[0, :, cols] = _subln_gate(oh, g_ref[0, :, cols], sw_ref[...], lam_init)


def _fused_attn_kernel(pt_ref, tab_ref, lam_ref, rel_ref, qbd_ref, *refs, n_parts, n_steps, n_new, tile, scale,
                       lam_init):
    n_kv = 4 * n_parts
    kv_refs = refs[:n_kv]
    (kn_ref, vn_ref, gs_ref, bias_last_ref, bias_new_ref, far_ref, sw_ref, perm_ref, expand_ref, mask_ref,
     q_ref, k_ref, v_ref, gp_ref, bias_ref, os_ref, op_ref,
     s_ref, sn_ref, ms_ref, w_ref, accs_ref, kscr_ref, mp_ref, lp_ref, accp_ref) = refs[n_kv:]
    _sample_attn_kernel(pt_ref, lam_ref, qbd_ref, *kv_refs, kn_ref, vn_ref, gs_ref, bias_last_ref, bias_new_ref,
                        far_ref, sw_ref, perm_ref, expand_ref, mask_ref, os_ref,
                        s_ref, sn_ref, ms_ref, w_ref, accs_ref, kscr_ref,
                        n_parts=n_parts, n_steps=n_steps, n_new=n_new, scale=scale, lam_init=lam_init)
    step = pl.program_id(0) * (2 * n_steps) + pl.program_id(1)

    @pl.when(tab_ref[0, step] == 1)
    def _():
        unit = (tab_ref[2, step], tab_ref[4, step], tab_ref[5, step], tab_ref[6, step])
        _prompt_attn_unit(unit, lam_ref, rel_ref, q_ref, k_ref, v_ref, gp_ref, bias_ref, sw_ref, op_ref,
                          mp_ref, lp_ref, accp_ref, tile=tile, scale=scale, lam_init=lam_init)


def _attention(page_table, lam, rel_bias, qbd, cache_k, cache_v, la, k_new, v_new, g_new,
               bias_last, bias_new, far, subln_w, q, k, v, g, bias_tiles, batch, seq, tile, n_new, lam_init):
    bs, n_pages = page_table.shape
    n_layers, n_phys, page, n_maps, d = cache_k.shape
    n_heads = n_maps // 2
    width = n_maps * d
    ncol = qbd.shape[2]
    hw = 2 * ATTN_HEAD_DIM
    nq = seq // tile
    pad_new = 16
    assert ncol % 2 == 0 and page + 1 >= FAR_DISTANCE and n_pages % 2 == 0 and n_new <= pad_new <= page
    n_steps = n_pages // 2
    cycle = 2 * n_steps
    perm, expand, mask = _sample_attn_constants(page, n_maps, n_heads, n_new)
    table = jnp.asarray(_prompt_attn_schedule(batch, n_heads, nq, bs, cycle))
    ck = cache_k.reshape(n_layers, n_phys, page * n_maps, d)
    cv = cache_v.reshape(n_layers, n_phys, page * n_heads, hw)
    n_parts = SAMPLE_PAGE_PARTS
    assert page % (n_parts * SUBLANES) == 0

    def kmap(which, part):
        return lambda b, t, pt, tab: (la, pt[b, 2 * jnp.minimum(t, n_steps - 1) + which], part, 0)

    def vmap(which, part):
        return lambda b, t, pt, tab: (la, pt[b, 2 * jnp.maximum(t - n_steps, 0) + which], part, 0)

    once = pl.Buffered(1)
    kblock = (1, 1, ck.shape[2] // n_parts, d)
    vblock = (1, 1, cv.shape[2] // n_parts, hw)
    kv_specs = ([pl.BlockSpec(kblock, kmap(w, p)) for w in range(2) for p in range(n_parts)]
                + [pl.BlockSpec(vblock, vmap(w, p)) for w in range(2) for p in range(n_parts)])
    per_seq = lambda shape, **kw: pl.BlockSpec((1,) + shape, lambda b, t, pt, tab: (b, 0, 0), **kw)
    const = lambda shape: pl.BlockSpec(shape, lambda b, t, pt, tab: (0, 0), pipeline_mode=once)
    qmap = lambda b, t, pt, tab: (tab[1, b * cycle + t] * nq + tab[3, b * cycle + t], tab[2, b * cycle + t])
    kvmap = lambda b, t, pt, tab: (tab[1, b * cycle + t], tab[2, b * cycle + t])
    qspec = pl.BlockSpec((tile, hw), qmap)
    kvspec = pl.BlockSpec((seq, hw), kvmap)
    smem = pl.BlockSpec(memory_space=pltpu.SMEM)
    grid_spec = pltpu.PrefetchScalarGridSpec(
        num_scalar_prefetch=2,
        grid=(bs, cycle),
        in_specs=[smem, smem,
                  per_seq((width, ncol), pipeline_mode=once), *kv_specs,
                  per_seq((n_new, width)), per_seq((n_new, width)), per_seq((n_new, width)),
                  const((page, ncol)), const((page, ncol)), const((1, ncol)), const((1, hw)),
                  const(perm.shape), const(expand.shape), const(mask.shape),
                  qspec, kvspec, kvspec, qspec,
                  pl.BlockSpec((1, 2, tile, tile), lambda b, t, pt, tab: (tab[2, b * cycle + t], 0, 0, 0))],
        out_specs=[per_seq((n_new, width)), qspec],
        scratch_shapes=[pltpu.VMEM((n_steps, 2 * page, ncol), F32), pltpu.VMEM((pad_new, ncol), F32),
                        pltpu.VMEM((1, ncol), F32), pltpu.VMEM((2, ncol // 2), F32),
                        pltpu.VMEM((n_heads * n_new, hw), F32), pltpu.VMEM((2 * page, width), BF16),
                        pltpu.VMEM((2, tile, 1), F32), pltpu.VMEM((2, tile, 1), F32),
                        pltpu.VMEM((2, tile, hw), F32)],
    )
    return pl.pallas_call(
        functools.partial(_fused_attn_kernel, n_parts=n_parts, n_steps=n_steps, n_new=n_new, tile=tile,
                          scale=ATTN_HEAD_DIM ** -0.5, lam_init=lam_init),
        grid_spec=grid_spec,
        out_shape=[jax.ShapeDtypeStruct((bs, n_new, width), F32), jax.ShapeDtypeStruct(q.shape, BF16)],
        compiler_params=pltpu.CompilerParams(dimension_semantics=("arbitrary", "arbitrary"),
                                             vmem_limit_bytes=V7X_FUSED_ATTN_VMEM_LIMIT_BYTES),
        name="attention",
    )(page_table, table, lam, rel_bias, qbd, *([ck] * (2 * n_parts)), *([cv] * (2 * n_parts)), k_new, v_new, g_new,
      bias_last, bias_new, far, subln_w.reshape(1, hw), perm, expand, mask, q, k, v, g, bias_tiles)


def _block_diag_queries(q, bs, n_new):
    d = ATTN_HEAD_DIM
    n_maps = q.shape[1] // d
    n_heads = n_maps // 2
    q5 = q.reshape(bs, n_new, n_heads, 2, d)
    qt = jnp.transpose(q5, (0, 2, 3, 4, 1))
    eye_h = jnp.eye(n_heads, dtype=q.dtype)
    eye_j = jnp.eye(2, dtype=q.dtype)
    out = (qt[:, :, :, :, None, None, :] * eye_j[None, None, :, None, :, None, None]
           * eye_h[None, :, None, None, None, :, None])
    return out.reshape(bs, n_maps * d, 2 * n_heads * n_new)


def _conv_kernel(x_ref, st_ref, w_ref, b_ref, y_ref, tail_ref, pad_ref, *, width):
    t = pl.program_id(2)
    tt = x_ref.shape[0]
    lo = 8 - (width - 1)

    @pl.when(t == 0)
    def _():
        pad_ref[lo:8, :] = st_ref[0]

    x = x_ref[...]
    pad_ref[8:8 + tt, :] = x
    acc = b_ref[...] + x * w_ref[width - 1:width, :]
    for j in range(width - 1):
        acc = acc + pad_ref[lo + j:lo + j + tt, :] * w_ref[j:j + 1, :]
    y_ref[...] = _silu(acc)
    tail = x[tt - (width - 1):, :]
    pad_ref[lo:8, :] = tail
    tail_ref[0] = tail


def _matmul_conv_kernel(a_ref, w_ref, cw_ref, cb_ref, y_ref, tail_ref, pad_ref, *, width, tiles_per_seq):
    i = pl.program_id(1)
    tm = a_ref.shape[0]
    lo = 8 - (width - 1)

    @pl.when(i % tiles_per_seq == 0)
    def _():
        pad_ref[lo:8, :] = jnp.zeros((width - 1, pad_ref.shape[1]), F32)

    chunk = min(2 * LANES, pad_ref.shape[1])
    for c0 in range(0, pad_ref.shape[1], chunk):
        cols = slice(c0, c0 + chunk)
        x = jnp.dot(a_ref[...], w_ref[:, cols], preferred_element_type=F32)
        pad_ref[8:8 + tm, cols] = x
        acc = cb_ref[:, cols] + x * cw_ref[width - 1:width, cols]
        for j in range(width - 1):
            acc = acc + pad_ref[lo + j:lo + j + tm, cols] * cw_ref[j:j + 1, cols]
        y_ref[:, cols] = _silu(acc)
        tail = x[tm - (width - 1):, :]
        pad_ref[lo:8, cols] = tail
        tail_ref[0, :, cols] = tail


def _matmul_conv(a, w, col0, conv_w, conv_b, batch, seq, tm, tn):
    m, k = a.shape
    width, c = conv_w.shape
    tm = min(tm, seq)
    assert seq % tm == 0 and c % tn == 0 and col0 % tn == 0 and tm >= width - 1
    cb0 = col0 // tn
    tiles_per_seq = seq // tm
    return pl.pallas_call(
        functools.partial(_matmul_conv_kernel, width=width, tiles_per_seq=tiles_per_seq),
        grid=(c // tn, m // tm),
        in_specs=[pl.BlockSpec((tm, k), lambda j, i: (i, 0)),
                  pl.BlockSpec((k, tn), lambda j, i: (0, cb0 + j)),
                  pl.BlockSpec((width, tn), lambda j, i: (0, j)),
                  pl.BlockSpec((1, tn), lambda j, i: (0, j))],
        out_specs=[pl.BlockSpec((tm, tn), lambda j, i: (i, j)),
                   pl.BlockSpec((1, width - 1, tn), lambda j, i: (i // tiles_per_seq, 0, j))],
        out_shape=[jax.ShapeDtypeStruct((m, c), F32),
                   jax.ShapeDtypeStruct((batch, width - 1, c), F32)],
        scratch_shapes=[pltpu.VMEM((8 + tm, tn), F32)],
        compiler_params=_cparams("parallel", "arbitrary"),
        name="matmul_conv",
    )(a, w, conv_w, conv_b.reshape(1, c))


def _conv(xbc, conv_state, conv_w, conv_b, batch, seq):
    m, c = xbc.shape
    width = conv_w.shape[0]
    tt = min(seq, 256)
    tc = 1024
    assert seq % tt == 0 and c % tc == 0 and tt >= width - 1
    nt = seq // tt
    return pl.pallas_call(
        functools.partial(_conv_kernel, width=width),
        grid=(batch, c // tc, nt),
        in_specs=[pl.BlockSpec((tt, tc), lambda b, j, t: (b * nt + t, j)),
                  pl.BlockSpec((1, width - 1, tc), lambda b, j, t: (b, 0, j)),
                  pl.BlockSpec((width, tc), lambda b, j, t: (0, j)),
                  pl.BlockSpec((1, tc), lambda b, j, t: (0, j))],
        out_specs=[pl.BlockSpec((tt, tc), lambda b, j, t: (b * nt + t, j)),
                   pl.BlockSpec((1, width - 1, tc), lambda b, j, t: (b, 0, j))],
        out_shape=[jax.ShapeDtypeStruct((m, c), F32),
                   jax.ShapeDtypeStruct((batch, width - 1, c), F32)],
        scratch_shapes=[pltpu.VMEM((8 + tt, tc), F32)],
        compiler_params=_cparams("parallel", "parallel", "arbitrary"),
        name="ssd_conv",
    )(xbc, conv_state, conv_w, conv_b.reshape(1, c))


def _split3(v):
    hi = v.astype(BF16)
    r1 = v - hi.astype(F32)
    mid = r1.astype(BF16)
    lo = (r1 - mid.astype(F32)).astype(BF16)
    return hi, mid, lo


def _dt_kernel(x_ref, bias_ref, alog_ref, dt_ref, cum_ref, dec_ref, *maybe_parts_ref):
    q = x_ref.shape[0]
    x = x_ref[...] + bias_ref[...]
    dt = jnp.maximum(x, 0.0) + jnp.log1p(jnp.exp(-jnp.abs(x)))
    c = dt * (-jnp.exp(alog_ref[...]))
    row = lax.broadcasted_iota(jnp.int32, c.shape, 0)
    shift = 1
    while shift < q:
        c = c + jnp.where(row >= shift, pltpu.roll(c, shift, axis=0), 0.0)
        shift *= 2
    dt_ref[...] = dt
    cum_ref[...] = c
    dec_ref[0] = jnp.exp(c[q - 1:q, :])
    if maybe_parts_ref:
        (parts_ref,) = maybe_parts_ref
        w_end = jnp.exp(c[q - 1:q, :] - c) * dt
        for i, v in enumerate((c, jnp.exp(c), w_end)):
            for j, part in enumerate(_split3(v)):
                parts_ref[3 * i + j] = part


def _dt_prep(dt_raw, dt_bias, a_log, q, with_parts):
    m, nh = dt_raw.shape
    row = pl.BlockSpec((q, nh), lambda i: (i, 0))
    vec = pl.BlockSpec((1, nh), lambda i: (0, 0))
    out_specs = [row, row, pl.BlockSpec((1, 1, nh), lambda i: (i, 0, 0))]
    out_shape = [jax.ShapeDtypeStruct((m, nh), F32), jax.ShapeDtypeStruct((m, nh), F32),
                 jax.ShapeDtypeStruct((m // q, 1, nh), F32)]
    if with_parts:
        out_specs.append(pl.BlockSpec((9, q, nh), lambda i: (0, i, 0)))
        out_shape.append(jax.ShapeDtypeStruct((9, m, nh), BF16))
    return pl.pallas_call(
        _dt_kernel,
        grid=(m // q,),
        in_specs=[row, vec, vec],
        out_specs=out_specs,
        out_shape=out_shape,
        compiler_params=_cparams("parallel"),
        name="ssd_dt_prep",
    )(dt_raw, dt_bias.reshape(1, nh), a_log.reshape(1, nh))


def _ssd_kernel(*refs, has_init, n_chunks):
    if has_init:
        (x_ref, b_ref, c_ref, z_ref, dt_ref, cum_ref, dtt_ref, cumt_ref, dec_ref, dskip_ref, gw_ref, s0_ref,
         y_ref, sout_ref, state_ref, ys_ref, xw_ref) = refs
    else:
        (x_ref, b_ref, c_ref, z_ref, dt_ref, cum_ref, dtt_ref, cumt_ref, dec_ref, dskip_ref, gw_ref,
         y_ref, sout_ref, state_ref, ys_ref, xw_ref) = refs
    ck = pl.program_id(2)
    q = x_ref.shape[0]
    r_heads = dt_ref.shape[3]
    p = SSM_HEAD_DIM

    @pl.when(ck == 0)
    def _():
        if has_init:
            state_ref[...] = s0_ref[0, 0].T
        else:
            state_ref[...] = jnp.zeros_like(state_ref)

    x = x_ref[...]
    xb = x.astype(BF16)
    bb = b_ref[...].astype(BF16)
    cb_ = c_ref[...].astype(BF16)
    cb = lax.dot_general(cb_, bb, (((1,), (1,)), ((), ())), preferred_element_type=F32)
    cs = jnp.dot(cb_, state_ref[...].astype(BF16), preferred_element_type=F32)
    dt = dt_ref[0, 0]
    cum = cum_ref[0, 0]
    dtt = dtt_ref[0, 0]
    cumt = cumt_ref[0, 0]
    w_end = jnp.exp(cum[q - 1:q, :] - cum) * dt
    ecum = jnp.exp(cum)
    li = lax.broadcasted_iota(jnp.int32, (q, q), 0)
    si = lax.broadcasted_iota(jnp.int32, (q, q), 1)
    causal = li >= si
    for r in range(r_heads):
        cols = slice(r * p, (r + 1) * p)
        seg = cum[:, r:r + 1] - cumt[r:r + 1, :]
        decay = jnp.exp(jnp.where(causal, seg, NEG_INF))
        mh = (cb * decay * dtt[r:r + 1, :]).astype(BF16)
        ys_ref[:, cols] = (jnp.dot(mh, xb[:, cols], preferred_element_type=F32)
                           + cs[:, cols] * ecum[:, r:r + 1])
        xw_ref[:, cols] = (x[:, cols] * w_end[:, r:r + 1]).astype(BF16)
    ds = lax.dot_general(bb, xw_ref[...], (((0,), (0,)), ((), ())), preferred_element_type=F32)
    state_ref[...] = state_ref[...] * dec_ref[0, 0, 0] + ds

    y = ys_ref[...] + x * dskip_ref[...]
    y = y * _silu(z_ref[...])
    y = y * lax.rsqrt(jnp.mean(y * y, axis=-1, keepdims=True) + NORM_EPS) * gw_ref[...]
    y_ref[...] = y.astype(y_ref.dtype)

    @pl.when(ck == n_chunks - 1)
    def _():
        sout_ref[0, 0] = state_ref[...].T


def _ssd_scan(act, z, dt, cum, dec_last, d_skip, gnorm_w, state0, batch, seq, q, out_dtype):
    m = act.shape[0]
    inner = z.shape[1]
    g = SSM_GROUPS
    n = SSM_STATE
    gw_cols = inner // g
    r = gw_cols // SSM_HEAD_DIM
    nc = seq // q
    assert n == LANES and gw_cols % LANES == 0
    dt4 = jnp.transpose(dt.reshape(batch, seq, g, r), (0, 2, 1, 3))
    cum4 = jnp.transpose(cum.reshape(batch, seq, g, r), (0, 2, 1, 3))
    dtt4 = jnp.transpose(dt4, (0, 1, 3, 2))
    cumt4 = jnp.transpose(cum4, (0, 1, 3, 2))
    dec5 = jnp.repeat(dec_last.reshape(batch, nc, g, 1, r), SSM_HEAD_DIM, axis=-1)
    dskip_row = jnp.repeat(d_skip, SSM_HEAD_DIM).reshape(1, inner)

    xcol0 = inner // gw_cols
    row_blk = lambda off: pl.BlockSpec((q, gw_cols), lambda b, gi, c: (b * nc + c, off + gi))
    bspec = pl.BlockSpec((q, n), lambda b, gi, c: (b * nc + c, inner // n + gi))
    cspec = pl.BlockSpec((q, n), lambda b, gi, c: (b * nc + c, inner // n + g + gi))
    tcol = pl.BlockSpec((1, 1, q, r), lambda b, gi, c: (b, gi, c, 0))
    trow = pl.BlockSpec((1, 1, r, q), lambda b, gi, c: (b, gi, 0, c))
    vec = pl.BlockSpec((1, gw_cols), lambda b, gi, c: (0, gi))
    sspec = pl.BlockSpec((1, 1, gw_cols, n), lambda b, gi, c: (b, gi, 0, 0))
    in_specs = [row_blk(0), bspec, cspec, row_blk(0), tcol, tcol, trow, trow,
                pl.BlockSpec((1, 1, 1, 1, gw_cols), lambda b, gi, c: (b, c, gi, 0, 0)), vec, vec]
    args = [act, act, act, z, dt4, cum4, dtt4, cumt4, dec5, dskip_row, gnorm_w.reshape(1, inner)]
    if state0 is not None:
        in_specs.append(sspec)
        args.append(state0.reshape(batch, g, gw_cols, n))
    del xcol0
    y, s_out = pl.pallas_call(
        functools.partial(_ssd_kernel, has_init=state0 is not None, n_chunks=nc),
        grid=(batch, g, nc),
        in_specs=in_specs,
        out_specs=[row_blk(0), sspec],
        out_shape=[jax.ShapeDtypeStruct((m, inner), out_dtype),
                   jax.ShapeDtypeStruct((batch, g, gw_cols, n), F32)],
        scratch_shapes=[pltpu.VMEM((n, gw_cols), F32), pltpu.VMEM((q, gw_cols), F32),
                        pltpu.VMEM((q, gw_cols), BF16)],
        compiler_params=_cparams("parallel", "parallel", "arbitrary"),
        name="ssd_scan_stateful",
    )(*args)
    return y, s_out


def _expansion_matrix(r_heads):
    p = SSM_HEAD_DIM
    e = np.zeros((9 * r_heads, r_heads * (LANES + 2 * p)), np.float32)
    for k in range(9):
        for r in range(r_heads):
            if k < 3:
                lo, w = r * LANES, LANES
            else:
                lo, w = r_heads * LANES + (k // 3 - 1) * r_heads * p + r * p, p
            e[k * r_heads + r, lo:lo + w] = 1.0
    return jnp.asarray(e, BF16)


def _ssd_chunk_kernel(x_ref, b_ref, c_ref, z_ref, parts_ref, dtt_ref, cumt_ref, e_ref, dec_ref, dskip_ref, gw_ref,
                      y_ref, sout_ref, state_ref, xl_ref, ys_ref, *, n_chunks):
    ck = pl.program_id(2)
    q = x_ref.shape[0]
    r_heads = dtt_ref.shape[2]
    p = SSM_HEAD_DIM

    @pl.when(ck == 0)
    def _():
        state_ref[...] = jnp.zeros_like(state_ref)

    x = x_ref[...]
    bb = b_ref[...].astype(BF16)
    cb_ = c_ref[...].astype(BF16)
    cb = lax.dot_general(cb_, bb, (((1,), (1,)), ((), ())), preferred_element_type=F32)
    cs = jnp.dot(cb_, state_ref[...].astype(BF16), preferred_element_type=F32)
    xl_ref[...] = jnp.dot(parts_ref[0, 0], e_ref[...], preferred_element_type=F32)
    dtt = dtt_ref[0, 0]
    cumt = cumt_ref[0, 0]
    li = lax.broadcasted_iota(jnp.int32, (q, q), 0)
    si = lax.broadcasted_iota(jnp.int32, (q, q), 1)
    causal = li >= si
    first = lax.broadcasted_iota(jnp.int32, (q, 2 * p), 1) < p
    for pair in range(r_heads // 2):
        mhs = []
        for r in (2 * pair, 2 * pair + 1):
            seg = xl_ref[:, r * LANES:(r + 1) * LANES] - cumt[r:r + 1, :]
            decay = jnp.exp(jnp.where(causal, seg, NEG_INF))
            mhs.append((cb * decay * dtt[r:r + 1, :]).astype(BF16))
        xp = x[:, pair * 2 * p:(pair + 1) * 2 * p]
        rhs = jnp.concatenate([jnp.where(first, xp, 0.0).astype(BF16),
                               jnp.where(first, 0.0, xp).astype(BF16)], axis=0)
        ys_ref[:, pair * 2 * p:(pair + 1) * 2 * p] = jnp.dot(jnp.concatenate(mhs, axis=1), rhs,
                                                             preferred_element_type=F32)
    c0 = r_heads * LANES
    ecum = xl_ref[:, c0:c0 + r_heads * p]
    w_end = xl_ref[:, c0 + r_heads * p:c0 + 2 * r_heads * p]
    xw = (x * w_end).astype(BF16)
    ds = lax.dot_general(bb, xw, (((0,), (0,)), ((), ())), preferred_element_type=F32)
    state_ref[...] = state_ref[...] * dec_ref[0, 0, 0] + ds

    y = ys_ref[...] + cs * ecum + x * dskip_ref[...]
    y = y * _silu(z_ref[...])
    y = y * lax.rsqrt(jnp.mean(y * y, axis=-1, keepdims=True) + NORM_EPS) * gw_ref[...]
    y_ref[...] = y.astype(y_ref.dtype)

    @pl.when(ck == n_chunks - 1)
    def _():
        sout_ref[0, 0] = state_ref[...].T


def _ssd_scan_chunked(act, z, dt, cum, dec_last, parts, d_skip, gnorm_w, batch, seq, q, out_dtype):
    m = act.shape[0]
    inner = z.shape[1]
    g = SSM_GROUPS
    n = SSM_STATE
    gw_cols = inner // g
    r = gw_cols // SSM_HEAD_DIM
    nc = seq // q
    assert n == LANES and q == LANES and gw_cols % LANES == 0 and r % 2 == 0
    dtt4 = jnp.transpose(dt.reshape(batch, seq, g, r), (0, 2, 3, 1))
    cumt4 = jnp.transpose(cum.reshape(batch, seq, g, r), (0, 2, 3, 1))
    parts4 = jnp.transpose(parts.reshape(9, batch, seq, g, r), (1, 3, 2, 0, 4)).reshape(batch, g, seq, 9 * r)
    dec5 = jnp.repeat(dec_last.reshape(batch, nc, g, 1, r), SSM_HEAD_DIM, axis=-1)
    dskip_row = jnp.repeat(d_skip, SSM_HEAD_DIM).reshape(1, inner)
    emat = _expansion_matrix(r)

    row_blk = pl.BlockSpec((q, gw_cols), lambda b, gi, c: (b * nc + c, gi))
    bspec = pl.BlockSpec((q, n), lambda b, gi, c: (b * nc + c, inner // n + gi))
    cspec = pl.BlockSpec((q, n), lambda b, gi, c: (b * nc + c, inner // n + g + gi))
    trow = pl.BlockSpec((1, 1, r, q), lambda b, gi, c: (b, gi, 0, c))
    vec = pl.BlockSpec((1, gw_cols), lambda b, gi, c: (0, gi))
    sspec = pl.BlockSpec((1, 1, gw_cols, n), lambda b, gi, c: (b, gi, 0, 0))
    y, s_out = pl.pallas_call(
        functools.partial(_ssd_chunk_kernel, n_chunks=nc),
        grid=(batch, g, nc),
        in_specs=[row_blk, bspec, cspec, row_blk,
                  pl.BlockSpec((1, 1, q, 9 * r), lambda b, gi, c: (b, gi, c, 0)), trow, trow,
                  pl.BlockSpec(emat.shape, lambda b, gi, c: (0, 0)),
                  pl.BlockSpec((1, 1, 1, 1, gw_cols), lambda b, gi, c: (b, c, gi, 0, 0)), vec, vec],
        out_specs=[row_blk, sspec],
        out_shape=[jax.ShapeDtypeStruct((m, inner), out_dtype),
                   jax.ShapeDtypeStruct((batch, g, gw_cols, n), F32)],
        scratch_shapes=[pltpu.VMEM((n, gw_cols), F32), pltpu.VMEM((q, emat.shape[1]), F32),
                        pltpu.VMEM((q, gw_cols), F32)],
        compiler_params=_cparams("parallel", "parallel", "arbitrary"),
        name="ssd_scan_chunked",
    )(act, act, act, z, parts4, dtt4, cumt4, emat, dec5, dskip_row, gnorm_w.reshape(1, inner))
    return y, s_out


def _attn_layer(hp, hs, w_in, w_out, lambda_qk_l, subln_w_l, rel_bias, cache_k, cache_v, la, page_table,
                batch, seq, bs, n_new, lam_init):
    width = w_out.shape[0]
    tile = min(seq, 512)
    assert seq % tile == 0 and tile + 1 >= FAR_DISTANCE
    page = cache_k.shape[2]

    bias_tiles = _bias_tiles(rel_bias, tile)
    bias_last, bias_new, far, lam = _sample_prep(rel_bias, lambda_qk_l, page, n_new, lam_init)

    (q,) = _matmul(hp, w_in, 0, width, [BF16], 512, 1024)
    k, kb = _matmul(hp, w_in, width, width, [F32, BF16], 512, 1024)
    v, vb = _matmul(hp, w_in, 2 * width, width, [F32, BF16], 512, 1024)
    (g,) = _matmul(hp, w_in, 3 * width, width, [F32], 512, 1024)

    (proj2,) = _matmul(hs, w_in, 0, 4 * width, [F32], 512, 1024)
    q2, k2, v2, g2 = (proj2[:, i * width:(i + 1) * width] for i in range(4))
    qbd = _block_diag_queries(q2.astype(BF16), bs, n_new)
    og2, og = _attention(page_table, lam, rel_bias, qbd, cache_k, cache_v, la,
                         k2.reshape(bs, n_new, width), v2.reshape(bs, n_new, width), g2.reshape(bs, n_new, width),
                         bias_last, bias_new, far, subln_w_l, q, kb, vb, g, bias_tiles, batch, seq, tile, n_new,
                         lam_init)
    (op,) = _matmul(og, w_out, 0, w_out.shape[1], [F32], 512, 1024)
    (os_,) = _matmul(og2.reshape(bs * n_new, width).astype(BF16), w_out, 0, w_out.shape[1], [F32], 512, 1024)
    return op, os_, k, v, k2, v2


def _ssd_layer(h, conv_state, ssm_state, w_in, conv_w, conv_b, dt_bias, a_log, d_skip, gnorm_w, w_out,
               batch, seq):
    inner = w_out.shape[0]
    conv_dim = conv_w.shape[1]
    n_heads = dt_bias.shape[0]
    q = SSD_CHUNK if seq % SSD_CHUNK == 0 else seq
    if conv_state is None:
        (z,) = _matmul(h, w_in, 0, inner, [F32], 512, 1024)
        (dt_raw,) = _matmul(h, w_in, inner + conv_dim, n_heads, [F32], 512, n_heads)
        act, conv_out = _matmul_conv(h, w_in, inner, conv_w, conv_b, batch, seq, 512, 1024)
    else:
        total = w_in.shape[1]
        tn = max([t for t in range(LANES, 1024 + 1, LANES) if total % t == 0], default=total)
        (proj,) = _matmul(h, w_in, 0, total, [F32], 512, tn)
        z, xbc, dt_raw = proj[:, :inner], proj[:, inner:inner + conv_dim], proj[:, inner + conv_dim:]
        act, conv_out = _conv(xbc, conv_state, conv_w, conv_b, batch, seq)
    if ssm_state is None and q == LANES:
        dt, cum, dec_last, parts = _dt_prep(dt_raw, dt_bias, a_log, q, True)
        y, s_out = _ssd_scan_chunked(act, z, dt, cum, dec_last, parts, d_skip, gnorm_w, batch, seq, q, BF16)
    else:
        dt, cum, dec_last = _dt_prep(dt_raw, dt_bias, a_log, q, False)
        y, s_out = _ssd_scan(act, z, dt, cum, dec_last, d_skip, gnorm_w, ssm_state, batch, seq, q, F32)
    (o,) = _matmul(y.astype(BF16), w_out, 0, w_out.shape[1], [F32], 512, 512)
    return o, conv_out, s_out.reshape(batch, n_heads, SSM_HEAD_DIM, SSM_STATE)


def kernel(x_prompt, x_sample, cache_k, cache_v, page_table, state_conv, state_ssm, norm_pre, norm_post, rel_bias,
           w_attn_in, lambda_qk, subln_w, w_attn_out, w_ssm_in, conv_w, conv_b, dt_bias, a_log, d_skip, gnorm_w,
           w_ssm_out):
    bp, sp, dm = x_prompt.shape
    bs, ss, _ = x_sample.shape
    depth = norm_pre.shape[0]
    xp = x_prompt.reshape(bp * sp, dm)
    xs = x_sample.reshape(bs * ss, dm)
    kp_l, vp_l, ks_l, vs_l, cp_l, sp_l, cs_l, ss_l = [], [], [], [], [], [], [], []
    hp = _rmsnorm(xp, norm_pre[0], NORM_EPS, BF16)
    hs = _rmsnorm(xs, norm_pre[0], NORM_EPS, BF16)
    for i in range(depth):
        if i % N_MIXERS == 0:
            la = i // N_MIXERS
            lam_init = 0.8 - 0.6 * math.exp(-0.3 * i)
            op, os_, k, v, k2, v2 = _attn_layer(
                hp, hs, w_attn_in[la].astype(BF16), w_attn_out[la].astype(BF16), lambda_qk[la], subln_w[la],
                rel_bias, cache_k, cache_v, la, page_table, bp, sp, bs, ss, lam_init)
            n_maps = k.shape[1] // ATTN_HEAD_DIM
            kp_l.append(k.reshape(bp, sp, n_maps, ATTN_HEAD_DIM))
            vp_l.append(v.reshape(bp, sp, n_maps // 2, 2 * ATTN_HEAD_DIM))
            ks_l.append(k2.reshape(bs, ss, n_maps, ATTN_HEAD_DIM))
            vs_l.append(v2.reshape(bs, ss, n_maps // 2, 2 * ATTN_HEAD_DIM))
        else:
            ls = i // N_MIXERS
            wts = (w_ssm_in[ls].astype(BF16), conv_w[ls], conv_b[ls], dt_bias[ls], a_log[ls], d_skip[ls],
                   gnorm_w[ls], w_ssm_out[ls].astype(BF16))
            op, cpn, spn = _ssd_layer(hp, None, None, *wts, bp, sp)
            os_, csn, ssn = _ssd_layer(hs, state_conv[ls], state_ssm[ls], *wts, bs, ss)
            cp_l.append(cpn)
            sp_l.append(spn)
            cs_l.append(csn)
            ss_l.append(ssn)
        w_next = norm_pre[i + 1] if i + 1 < depth else None
        xp, hp = _residual_norm(xp, op, norm_post[i], NORM_EPS, w_next)
        xs, hs = _residual_norm(xs, os_, norm_post[i], NORM_EPS, w_next)
    return (xp.reshape(bp, sp, dm), xs.reshape(bs, ss, dm),
            jnp.stack(kp_l), jnp.stack(vp_l), jnp.stack(ks_l), jnp.stack(vs_l),
            jnp.stack(cp_l), jnp.stack(sp_l), jnp.stack(cs_l), jnp.stack(ss_l))
```
